```python
import jax, jax.numpy as jnp
from jax import lax
import numpy as np

D_MODEL = 1024
BATCH = 4
SEQ = 4096
DEPTH = 4
DEC_BATCH = 128
DEC_SEQ = 8
PAST_LEN = 2048
PAGE_SIZE = 128

N_A_LAYERS = DEPTH // 2
N_B_LAYERS = DEPTH - N_A_LAYERS
HEAD_DIM = 64
N_HEADS = D_MODEL // HEAD_DIM
CONV_WIDTH = 31
SB_BLOCK = 128
SB_BIAS_INIT = -6.0
PEER_HEADS = 8
PEER_TOPK = 16
PEER_NKEYS = 128
PEER_EXPERTS = PEER_NKEYS * PEER_NKEYS
PEER_DQ = D_MODEL // 4
PEER_DHALF = PEER_DQ // 2
PEER_CHUNK = 256
EPS = 1e-6

kernel_name = 'yoco_conformer_stickbreak_peer'


def rms_norm(x, g):
    x32 = x.astype(jnp.float32)
    y = x32 * lax.rsqrt(jnp.mean(x32 * x32, axis=-1, keepdims=True) + EPS)
    return (y * g.astype(jnp.float32)).astype(x.dtype)


def layer_norm(x, g, b):
    x32 = x.astype(jnp.float32)
    mu = jnp.mean(x32, axis=-1, keepdims=True)
    xc = x32 - mu
    var = jnp.mean(xc * xc, axis=-1, keepdims=True)
    return (xc * lax.rsqrt(var + EPS) * g.astype(jnp.float32) + b.astype(jnp.float32)).astype(x.dtype)


def ada_params(c, w, b):
    m = jax.nn.silu(c) @ w + b
    return jnp.split(m[:, None, :], 6, axis=-1)


def conv_module(h, hist, w_pw1, b_pw1, w_dw, b_dw, ln_g, ln_b, w_pw2, b_pw2):
    a = h @ w_pw1 + b_pw1
    glu = a[..., :D_MODEL] * jax.nn.sigmoid(a[..., D_MODEL:])
    full = jnp.concatenate([hist.astype(glu.dtype), glu], axis=1)
    y = lax.conv_general_dilated(full, w_dw[:, None, :].astype(full.dtype), window_strides=(1,),
                                 padding='VALID', dimension_numbers=('NWC', 'WIO', 'NWC'),
                                 feature_group_count=D_MODEL) + b_dw
    y = jax.nn.silu(layer_norm(y, ln_g, ln_b))
    return y @ w_pw2 + b_pw2, full[:, full.shape[1] - (CONV_WIDTH - 1):]


def stick_breaking(q, k, v, bias, q_off):
    tq = q.shape[1]
    bias32 = bias.astype(jnp.float32)[None, :, None, None]
    outs = []
    for start in range(0, tq, SB_BLOCK):
        end = min(start + SB_BLOCK, tq)
        kl = q_off + end
        z = jnp.einsum('bqhd,bkhd->bhqk', q[:, start:end], k[:, :kl]).astype(jnp.float32) * (HEAD_DIM ** -0.5) + bias32
        qpos = q_off + jnp.arange(start, end)
        kpos = jnp.arange(kl)
        mask = kpos[None, :] < qpos[:, None]
        log_rest = jnp.where(mask, jax.nn.log_sigmoid(-z), 0.0)
        after = lax.cumsum(log_rest, axis=3, reverse=True) - log_rest
        log_a = jnp.where(mask, jax.nn.log_sigmoid(z) + after, -jnp.inf)
        a = jnp.exp(log_a).astype(v.dtype)
        outs.append(jnp.einsum('bhqk,bkhd->bqhd', a, v[:, :kl]))
    return jnp.concatenate(outs, axis=1)


def peer(h, w_pq, sub_keys, expert_u, expert_v):
    b, t, d = h.shape
    xt = h.reshape(b * t, d)
    n = xt.shape[0]
    xt = jnp.pad(xt, ((0, (-n) % PEER_CHUNK), (0, 0))).reshape(-1, PEER_CHUNK, d)

    def block(xc):
        q = (xc @ w_pq).reshape(PEER_CHUNK, PEER_HEADS, 2, PEER_DHALF)
        s = jnp.einsum('nhpd,hpkd->nhpk', q, sub_keys).astype(jnp.float32)
        sv, si = lax.top_k(s, PEER_TOPK)
        cand = sv[:, :, 0, :, None] + sv[:, :, 1, None, :]
        cidx = si[:, :, 0, :, None] * PEER_NKEYS + si[:, :, 1, None, :]
        cv, ci = lax.top_k(cand.reshape(PEER_CHUNK, PEER_HEADS, -1), PEER_TOPK)
        eidx = jnp.take_along_axis(cidx.reshape(PEER_CHUNK, PEER_HEADS, -1), ci, axis=-1)
        g = jax.nn.softmax(cv, axis=-1)
        act = jax.nn.gelu(jnp.einsum('nd,nhkd->nhk', xc, expert_u[eidx]).astype(jnp.float32))
        w = (g * act).astype(xc.dtype)
        return jnp.einsum('nhk,nhkd->nd', w, expert_v[eidx])

    out = lax.map(block, xt).reshape(-1, d)[:n]
    return out.reshape(b, t, d)


def trunk(x, c, conv_hist, k_past, v_past, q_off, p):
    b, t, _ = x.shape
    new_hist = []
    k_new = v_new = k_all = v_all = None
    for l in range(DEPTH):
        sh1, sc1, g1, sh2, sc2, g2 = ada_params(c, p['w_ada'][l], p['b_ada'][l])
        h = rms_norm(x, p['norm_mix'][l]) * (1 + sc1) + sh1
        if l < N_A_LAYERS:
            hist = jnp.zeros((b, CONV_WIDTH - 1, D_MODEL), x.dtype) if conv_hist is None else conv_hist[l]
            out, nh = conv_module(h, hist, p['w_pw1'][l], p['b_pw1'][l], p['w_dw'][l], p['b_dw'][l],
                                  p['ln_g'][l], p['ln_b'][l], p['w_pw2'][l], p['b_pw2'][l])
            new_hist.append(nh)
        else:
            if l == N_A_LAYERS:
                hk = rms_norm(x, p['norm_kv'])
                k_new = (hk @ p['w_k']).reshape(b, t, N_HEADS, HEAD_DIM)
                v_new = (hk @ p['w_v']).reshape(b, t, N_HEADS, HEAD_DIM)
                k_all = k_new if k_past is None else jnp.concatenate([k_past.astype(k_new.dtype), k_new], axis=1)
                v_all = v_new if v_past is None else jnp.concatenate([v_past.astype(v_new.dtype), v_new], axis=1)
            j = l - N_A_LAYERS
            q = (h @ p['w_q'][j]).reshape(b, t, N_HEADS, HEAD_DIM)
            att = stick_breaking(q, k_all, v_all, p['b_sb'][j], q_off)
            out = att.reshape(b, t, N_HEADS * HEAD_DIM) @ p['w_o'][j]
        x = x + g1 * out
        h2 = rms_norm(x, p['norm_ffn'][l]) * (1 + sc2) + sh2
        x = x + g2 * peer(h2, p['w_pq'][l], p['sub_keys'][l], p['expert_u'][l], p['expert_v'][l])
    return rms_norm(x, p['final_norm']), jnp.stack(new_hist), k_new, v_new


def setup_inputs(seed: int = 0) -> dict:
    key = jax.random.key(seed)
    ks = iter(jax.random.split(key, 40))

    def nrm(shape, s):
        return jax.random.normal(next(ks), shape, jnp.float32) * s

    n_pages = PAST_LEN // PAGE_SIZE
    n_used = DEC_BATCH * n_pages
    n_pool = n_used + n_used // 4
    hd = N_HEADS * HEAD_DIM
    return {
        'x_prompt': nrm((BATCH, SEQ, D_MODEL), 1.0),
        'x_sample': nrm((DEC_BATCH, DEC_SEQ, D_MODEL), 1.0),
        'cache_k': nrm((n_pool, PAGE_SIZE, N_HEADS, HEAD_DIM), 1.0),
        'cache_v': nrm((n_pool, PAGE_SIZE, N_HEADS, HEAD_DIM), 1.0),
        'state_conv': nrm((N_A_LAYERS, DEC_BATCH, CONV_WIDTH - 1, D_MODEL), 0.5),
        'page_table': jax.random.permutation(next(ks), n_pool)[:n_used].reshape(DEC_BATCH, n_pages).astype(jnp.int32),
        'c_prompt': nrm((BATCH, D_MODEL), 1.0),
        'c_sample': nrm((DEC_BATCH, D_MODEL), 1.0),
        'w_ada': nrm((DEPTH, D_MODEL, 6 * D_MODEL), 0.5 * D_MODEL ** -0.5),
        'b_ada': nrm((DEPTH, 6 * D_MODEL), 0.01),
        'norm_mix': 1.0 + nrm((DEPTH, D_MODEL), 0.02),
        'norm_ffn': 1.0 + nrm((DEPTH, D_MODEL), 0.02),
        'w_pw1': nrm((N_A_LAYERS, D_MODEL, 2 * D_MODEL), D_MODEL ** -0.5),
        'b_pw1': nrm((N_A_LAYERS, 2 * D_MODEL), 0.01),
        'w_dw': nrm((N_A_LAYERS, CONV_WIDTH, D_MODEL), CONV_WIDTH ** -0.5),
        'b_dw': nrm((N_A_LAYERS, D_MODEL), 0.01),
        'ln_g': 1.0 + nrm((N_A_LAYERS, D_MODEL), 0.02),
        'ln_b': nrm((N_A_LAYERS, D_MODEL), 0.01),
        'w_pw2': nrm((N_A_LAYERS, D_MODEL, D_MODEL), D_MODEL ** -0.5),
        'b_pw2': nrm((N_A_LAYERS, D_MODEL), 0.01),
        'norm_kv': 1.0 + nrm((D_MODEL,), 0.02),
        'w_k': nrm((D_MODEL, hd), D_MODEL ** -0.5),
        'w_v': nrm((D_MODEL, hd), D_MODEL ** -0.5),
        'w_q': nrm((N_B_LAYERS, D_MODEL, hd), D_MODEL ** -0.5),
        'b_sb': SB_BIAS_INIT + nrm((N_B_LAYERS, N_HEADS), 0.1),
        'w_o': nrm((N_B_LAYERS, hd, D_MODEL), hd ** -0.5),
        'w_pq': nrm((DEPTH, D_MODEL, PEER_HEADS * PEER_DQ), D_MODEL ** -0.5),
        'sub_keys': nrm((DEPTH, PEER_HEADS, 2, PEER_NKEYS, PEER_DHALF), PEER_DHALF ** -0.5),
        'expert_u': nrm((DEPTH, PEER_EXPERTS, D_MODEL), D_MODEL ** -0.5),
        'expert_v': nrm((DEPTH, PEER_EXPERTS, D_MODEL), 0.1),
        'final_norm': 1.0 + nrm((D_MODEL,), 0.02),
    }


def reference(x_prompt, x_sample, cache_k, cache_v, state_conv, page_table, c_prompt, c_sample,
              w_ada, b_ada, norm_mix, norm_ffn, w_pw1, b_pw1, w_dw, b_dw, ln_g, ln_b, w_pw2, b_pw2,
              norm_kv, w_k, w_v, w_q, b_sb, w_o, w_pq, sub_keys, expert_u, expert_v, final_norm):
    p = {'w_ada': w_ada, 'b_ada': b_ada, 'norm_mix': norm_mix, 'norm_ffn': norm_ffn,
         'w_pw1': w_pw1, 'b_pw1': b_pw1, 'w_dw': w_dw, 'b_dw': b_dw, 'ln_g': ln_g, 'ln_b': ln_b,
         'w_pw2': w_pw2, 'b_pw2': b_pw2, 'norm_kv': norm_kv, 'w_k': w_k, 'w_v': w_v,
         'w_q': w_q, 'b_sb': b_sb, 'w_o': w_o, 'w_pq': w_pq, 'sub_keys': sub_keys,
         'expert_u': expert_u, 'expert_v': expert_v, 'final_norm': final_norm}
    y_prompt, conv_prompt, k_prompt, v_prompt = trunk(x_prompt, c_prompt, None, None, None, 0, p)
    n_seq, n_pages = page_table.shape
    k_past = cache_k[page_table].reshape(n_seq, n_pages * PAGE_SIZE, N_HEADS, HEAD_DIM)
    v_past = cache_v[page_table].reshape(n_seq, n_pages * PAGE_SIZE, N_HEADS, HEAD_DIM)
    y_sample, conv_sample, k_sample, v_sample = trunk(x_sample, c_sample, state_conv, k_past, v_past,
                                                      n_pages * PAGE_SIZE, p)
    return (y_prompt, y_sample, conv_prompt, conv_sample, k_prompt, v_prompt, k_sample, v_sample)
```

```python
import functools

import jax
import jax.numpy as jnp
from jax import lax
from jax.experimental import pallas as pl
from jax.experimental.pallas import tpu as pltpu

F32 = jnp.float32
BF16 = jnp.bfloat16

EPS = 1e-6
HEAD_DIM = 64
CONV_WIDTH = 31
PAGE_SIZE = 128
PEER_HEADS = 8
PEER_TOPK = 16
PEER_NKEYS = 128
LANES = 128
HALO = 32
VMEM_LIMIT = 56 * 1024 * 1024
NEG_INF = float("-inf")


def _params(sem, vmem=VMEM_LIMIT):
    return pltpu.CompilerParams(dimension_semantics=sem, vmem_limit_bytes=vmem)


def _dot(a, b):
    return jnp.dot(a, b, preferred_element_type=F32)


def _dot_nt(a, b):
    return lax.dot_general(a, b, (((1,), (1,)), ((), ())), preferred_element_type=F32)


def _rms(x, g):
    return x * lax.rsqrt(jnp.mean(x * x, axis=-1, keepdims=True) + EPS) * g


def _softplus(z):
    return jnp.maximum(z, 0.0) + jnp.log1p(jnp.exp(-jnp.abs(z)))


def _split_bf16(x):
    hi = x.astype(BF16)
    lo = (x - hi.astype(F32)).astype(BF16)
    return hi, lo


def _ada_kernel(c_ref, w_ref, b_ref, o_ref):
    c = c_ref[...]
    s = (c * jax.nn.sigmoid(c)).astype(BF16)
    o_ref[0] = _dot(s, w_ref[0].astype(BF16)) + b_ref[0]


def _ada(c_all, w_ada, b_ada):
    depth, d, n6 = w_ada.shape
    rows = c_all.shape[0]
    tn = 1024
    return pl.pallas_call(
        _ada_kernel,
        out_shape=jax.ShapeDtypeStruct((depth, rows, n6), F32),
        grid=(depth, n6 // tn),
        in_specs=[
            pl.BlockSpec((rows, d), lambda l, j: (0, 0)),
            pl.BlockSpec((1, d, tn), lambda l, j: (l, 0, j)),
            pl.BlockSpec((1, 1, tn), lambda l, j: (l, 0, j)),
        ],
        out_specs=pl.BlockSpec((1, rows, tn), lambda l, j: (l, 0, j)),
        compiler_params=_params(("parallel", "parallel")),
        name="ada",
    )(c_all, w_ada, b_ada.reshape(depth, 1, n6))


class _Group:
    def __init__(self, mod, per_row, rows_per_batch):
        self.mod = mod
        self.per_row = per_row
        self.rows_per_batch = rows_per_batch

    def spec(self, k, tm, d):
        if self.per_row:
            return pl.BlockSpec((1, tm, d), lambda i, *_: (k, i, 0))
        tiles = self.rows_per_batch // tm
        return pl.BlockSpec((1, 1, d), lambda i, *_: ((i // tiles) * 6 + k, 0, 0))


def _nml_kernel(*refs, has_mod, has_bias, epi, scale):
    it = iter(refs)
    x_ref, g_ref = next(it), next(it)
    sc_ref = sh_ref = b_ref = None
    if has_mod:
        sc_ref, sh_ref = next(it), next(it)
    w_ref = next(it)
    if has_bias:
        b_ref = next(it)
    outs = list(it)
    y = _rms(x_ref[...], g_ref[...])
    if has_mod:
        y = y * (1.0 + sc_ref[0]) + sh_ref[0]
    a = _dot(y.astype(BF16), w_ref[...])
    if has_bias:
        a = a + b_ref[...]
    d = x_ref.shape[1]
    if epi == "glu":
        outs[0][...] = a[:, :d] * jax.nn.sigmoid(a[:, d:])
    elif epi == "scale":
        outs[0][...] = a * scale
    else:
        k, v = a[:, :d], a[:, d:]
        outs[0][...] = k
        outs[1][...] = v
        outs[2][...] = k.astype(BF16)
        outs[3][...] = v.astype(BF16)


def _nml(x, g, w_bf16, *, group=None, mod_k=None, bias=None, epi, scale=1.0, tm=256, name):
    n, d = x.shape
    nout = w_bf16.shape[1]
    args = [x, g.reshape(1, d)]
    specs = [pl.BlockSpec((tm, d), lambda i: (i, 0)), pl.BlockSpec((1, d), lambda i: (0, 0))]
    if group is not None:
        args += [group.mod, group.mod]
        specs += [group.spec(mod_k[0], tm, d), group.spec(mod_k[1], tm, d)]
    args.append(w_bf16)
    specs.append(pl.BlockSpec((d, nout), lambda i: (0, 0)))
    if bias is not None:
        args.append(bias.reshape(1, nout))
        specs.append(pl.BlockSpec((1, nout), lambda i: (0, 0)))
    tile = pl.BlockSpec((tm, d), lambda i: (i, 0))
    if epi == "kv":
        out_shape = [jax.ShapeDtypeStruct((n, d), F32)] * 2 + [jax.ShapeDtypeStruct((n, d), BF16)] * 2
        out_specs = [tile] * 4
    else:
        out_shape = jax.ShapeDtypeStruct((n, d), F32)
        out_specs = tile
    return pl.pallas_call(
        functools.partial(_nml_kernel, has_mod=group is not None, has_bias=bias is not None,
                          epi=epi, scale=scale),
        out_shape=out_shape, grid=(n // tm,), in_specs=specs, out_specs=out_specs,
        compiler_params=_params(("parallel",)), name=name,
    )(*args)


def _gated_linear_kernel(a_ref, w_ref, x_ref, gate_ref, o_ref):
    o_ref[...] = x_ref[...] + gate_ref[0] * _dot(a_ref[...].astype(BF16), w_ref[...])


def _gated_linear(a, w_bf16, x, group, gate_k, *, tm=256, name):
    n, d = x.shape
    tile = pl.BlockSpec((tm, d), lambda i: (i, 0))
    return pl.pallas_call(
        _gated_linear_kernel,
        out_shape=jax.ShapeDtypeStruct((n, d), F32), grid=(n // tm,),
        in_specs=[tile, pl.BlockSpec((d, d), lambda i: (0, 0)), tile, group.spec(gate_k, tm, d)],
        out_specs=tile, compiler_params=_params(("parallel",)), name=name,
    )(a, w_bf16, x, group.mod)


def _conv_tail(y, lng_ref, lnb_ref, w2_ref, b2_ref, x_ref, gate_ref, o_ref):
    mu = jnp.mean(y, axis=-1, keepdims=True)
    yc = y - mu
    var = jnp.mean(yc * yc, axis=-1, keepdims=True)
    yn = yc * lax.rsqrt(var + EPS) * lng_ref[...] + lnb_ref[...]
    act = yn * jax.nn.sigmoid(yn)
    out = _dot(act.astype(BF16), w2_ref[...]) + b2_ref[...]
    o_ref[...] = x_ref[...] + gate_ref[0] * out


def _conv_prompt_kernel(glu_ref, halo_ref, wdw_ref, bdw_ref, lng_ref, lnb_ref, w2_ref, b2_ref,
                        x_ref, gate_ref, o_ref, buf_ref, y_ref):
    t = pl.program_id(1)
    tm, d = x_ref.shape
    halo = halo_ref[0]
    buf_ref[0:HALO, :] = jnp.where(t > 0, halo, jnp.zeros_like(halo))
    buf_ref[HALO:HALO + tm, :] = glu_ref[0]
    first = HALO - (CONV_WIDTH - 1)
    for c in range(d // LANES):
        cs = slice(c * LANES, (c + 1) * LANES)
        acc = jnp.broadcast_to(bdw_ref[:, cs], (tm, LANES))
        for k in range(CONV_WIDTH):
            acc = acc + wdw_ref[k:k + 1, cs] * buf_ref[first + k:first + k + tm, cs]
        y_ref[:, cs] = acc
    _conv_tail(y_ref[...], lng_ref, lnb_ref, w2_ref, b2_ref, x_ref, gate_ref, o_ref)


def _conv_prompt(glu, x, lp, group, batch, seq, *, tm=256):
    n, d = x.shape
    tiles = seq // tm
    hb = tm // HALO
    glu3 = glu.reshape(batch, seq, d)
    row = lambda b, t: (0, 0)
    return pl.pallas_call(
        _conv_prompt_kernel,
        out_shape=jax.ShapeDtypeStruct((n, d), F32), grid=(batch, tiles),
        in_specs=[
            pl.BlockSpec((1, tm, d), lambda b, t: (b, t, 0)),
            pl.BlockSpec((1, HALO, d), lambda b, t: (b, jnp.maximum(t * hb - 1, 0), 0)),
            pl.BlockSpec((HALO, d), row), pl.BlockSpec((1, d), row),
            pl.BlockSpec((1, d), row), pl.BlockSpec((1, d), row),
            pl.BlockSpec((d, d), row), pl.BlockSpec((1, d), row),
            pl.BlockSpec((tm, d), lambda b, t: (b * tiles + t, 0)),
            pl.BlockSpec((1, 1, d), lambda b, t: (b * 6 + 2, 0, 0)),
        ],
        out_specs=pl.BlockSpec((tm, d), lambda b, t: (b * tiles + t, 0)),
        scratch_shapes=[pltpu.VMEM((HALO + tm, d), F32), pltpu.VMEM((tm, d), F32)],
        compiler_params=_params(("parallel", "arbitrary")), name="conv_prompt",
    )(glu3, glu3, lp["w_dw"], lp["b_dw"], lp["ln_g"], lp["ln_b"], lp["w_pw2"], lp["b_pw2"], x, group.mod)


def _conv_sample_kernel(hist_ref, glu_ref, wdw_ref, bdw_ref, lng_ref, lnb_ref, w2_ref, b2_ref,
                        x_ref, gate_ref, o_ref, buf_ref):
    sb, tq, d = glu_ref.shape
    nh = CONV_WIDTH - 1
    buf_ref[:, 0:nh, :] = hist_ref[...]
    buf_ref[:, nh:nh + tq, :] = glu_ref[...]
    acc = jnp.broadcast_to(bdw_ref[...].reshape(1, 1, d), (sb, tq, d))
    for k in range(CONV_WIDTH):
        acc = acc + wdw_ref[k:k + 1, :].reshape(1, 1, d) * buf_ref[:, k:k + tq, :]
    _conv_tail(acc.reshape(sb * tq, d), lng_ref, lnb_ref, w2_ref, b2_ref, x_ref, gate_ref, o_ref)


def _conv_sample(glu, hist, x, lp, group, nseq, tq, *, sb=16):
    n, d = x.shape
    nh = CONV_WIDTH - 1
    row = lambda i: (0, 0)
    return pl.pallas_call(
        _conv_sample_kernel,
        out_shape=jax.ShapeDtypeStruct((n, d), F32), grid=(nseq // sb,),
        in_specs=[
            pl.BlockSpec((sb, nh, d), lambda i: (i, 0, 0)),
            pl.BlockSpec((sb, tq, d), lambda i: (i, 0, 0)),
            pl.BlockSpec((HALO, d), row), pl.BlockSpec((1, d), row),
            pl.BlockSpec((1, d), row), pl.BlockSpec((1, d), row),
            pl.BlockSpec((d, d), row), pl.BlockSpec((1, d), row),
            pl.BlockSpec((sb * tq, d), lambda i: (i, 0)),
            pl.BlockSpec((1, sb * tq, d), lambda i: (2, i, 0)),
        ],
        out_specs=pl.BlockSpec((sb * tq, d), lambda i: (i, 0)),
        scratch_shapes=[pltpu.VMEM((sb, nh + tq + 2, d), F32)],
        compiler_params=_params(("parallel",)), name="conv_sample",
    )(hist, glu.reshape(nseq, tq, d), lp["w_dw"], lp["b_dw"], lp["ln_g"], lp["ln_b"],
      lp["w_pw2"], lp["b_pw2"], x, group.mod)


def _suffix_matrix(n):
    j = lax.broadcasted_iota(jnp.int32, (n, n), 0)
    s = lax.broadcasted_iota(jnp.int32, (n, n), 1)
    return jnp.where(j > s, 1.0, 0.0).astype(BF16)


def _sb_block(z, mask, carry, m_suffix, ones):
    sp = _softplus(z)
    lr = -sp if mask is None else jnp.where(mask, -sp, 0.0)
    hi, lo = _split_bf16(lr)
    after = _dot(hi, m_suffix) + _dot(lo, m_suffix)
    reps = z.shape[1] // LANES
    carry_full = carry if reps == 1 else jnp.concatenate([carry] * reps, axis=1)
    a = jnp.exp(z + lr + after + carry_full)
    if mask is not None:
        a = jnp.where(mask, a, 0.0)
    new_carry = carry + _dot(hi, ones) + _dot(lo, ones)
    return a, new_carry


def _attn_prompt_kernel(bias_ref, q_ref, k_ref, v_ref, o_ref, carry_ref, acc_ref):
    hp = pl.program_id(1)
    qi = pl.program_id(2)
    tq = q_ref.shape[1]
    q2 = q_ref[0]
    lane = lax.broadcasted_iota(jnp.int32, (tq, LANES), 1)
    rel = (lax.broadcasted_iota(jnp.int32, (tq, tq), 0) - lax.broadcasted_iota(jnp.int32, (tq, tq), 1))
    m_suffix = _suffix_matrix(tq)
    ones = jnp.ones((tq, LANES), BF16)
    qm = [jnp.where((lane >= HEAD_DIM * e) & (lane < HEAD_DIM * (e + 1)), q2, 0.0).astype(BF16)
          for e in range(2)]
    bias = [bias_ref[2 * hp + e] for e in range(2)]
    carry_ref[...] = jnp.zeros_like(carry_ref)
    acc_ref[...] = jnp.zeros_like(acc_ref)

    def body(step, _):
        kb = qi - step
        ks = pl.multiple_of(kb * tq, tq)
        kblk = k_ref[0, pl.ds(ks, tq), :]
        vblk = v_ref[0, pl.ds(ks, tq), :]
        mask = rel > (kb - qi) * tq
        for e in range(2):
            z = _dot_nt(qm[e], kblk) + bias[e]
            a, carry = _sb_block(z, mask, carry_ref[e], m_suffix, ones)
            carry_ref[e] = carry
            acc_ref[e] += _dot(a.astype(BF16), vblk)
        return 0

    lax.fori_loop(0, qi + 1, body, 0)
    o_ref[0] = jnp.where(lane < HEAD_DIM, acc_ref[0], acc_ref[1])


def _attn_prompt(q, kb, vb, bias, batch, seq, *, tq=256):
    n, d = q.shape
    q3, k3, v3 = (a.reshape(batch, seq, d) for a in (q, kb, vb))
    hpairs = d // LANES
    out = pl.pallas_call(
        _attn_prompt_kernel,
        out_shape=jax.ShapeDtypeStruct((batch, seq, d), F32),
        grid=(batch, hpairs, seq // tq),
        in_specs=[
            pl.BlockSpec(memory_space=pltpu.SMEM),
            pl.BlockSpec((1, tq, LANES), lambda b, h, i: (b, i, h)),
            pl.BlockSpec((1, seq, LANES), lambda b, h, i: (b, 0, h)),
            pl.BlockSpec((1, seq, LANES), lambda b, h, i: (b, 0, h)),
        ],
        out_specs=pl.BlockSpec((1, tq, LANES), lambda b, h, i: (b, i, h)),
        scratch_shapes=[pltpu.VMEM((2, tq, LANES), F32), pltpu.VMEM((2, tq, LANES), F32)],
        compiler_params=_params(("parallel", "parallel", "arbitrary")), name="attn_prompt",
    )(bias, q3, k3, v3)
    return out.reshape(n, d)


def _attn_sample_kernel(pt_ref, bias_ref, q_ref, kn_ref, vn_ref, *rest, pps):
    kp_refs, vp_refs = rest[:pps], rest[pps:2 * pps]
    o_ref, qbd_ref, knp_ref, vnp_ref, carry_ref, acc_ref = rest[2 * pps:]
    g = pl.program_id(1)
    tq, d = q_ref.shape[1], q_ref.shape[2]
    rows = (d // HEAD_DIM) * tq
    m_suffix = _suffix_matrix(PAGE_SIZE)
    ones = jnp.ones((PAGE_SIZE, LANES), BF16)
    bias = bias_ref[...]

    @pl.when(g == 0)
    def _():
        r = lax.broadcasted_iota(jnp.int32, (rows, d), 0)
        c = lax.broadcasted_iota(jnp.int32, (rows, d), 1)
        qt = jnp.concatenate([q_ref[0]] * (rows // tq), axis=0)
        qbd_ref[...] = jnp.where(r // tq == c // HEAD_DIM, qt, 0.0).astype(BF16)
        knp_ref[...] = jnp.zeros_like(knp_ref)
        vnp_ref[...] = jnp.zeros_like(vnp_ref)
        knp_ref[0:tq, :] = kn_ref[0]
        vnp_ref[0:tq, :] = vn_ref[0]
        z = _dot_nt(qbd_ref[...], knp_ref[...].astype(BF16)) + bias
        rr = lax.broadcasted_iota(jnp.int32, (rows, PAGE_SIZE), 0)
        cc = lax.broadcasted_iota(jnp.int32, (rows, PAGE_SIZE), 1)
        mask = cc < rr % tq
        a, carry = _sb_block(z, mask, jnp.zeros((rows, LANES), F32), m_suffix, ones)
        carry_ref[...] = carry
        acc_ref[...] = _dot(a.astype(BF16), vnp_ref[...].astype(BF16))

    qbd = qbd_ref[...]
    for i in reversed(range(pps)):
        z = _dot_nt(qbd, kp_refs[i][0].astype(BF16)) + bias
        a, carry = _sb_block(z, None, carry_ref[...], m_suffix, ones)
        carry_ref[...] = carry
        acc_ref[...] += _dot(a.astype(BF16), vp_refs[i][0].astype(BF16))

    @pl.when(g == pl.num_programs(1) - 1)
    def _():
        c = lax.broadcasted_iota(jnp.int32, (tq, d), 1)
        out = jnp.zeros((tq, d), F32)
        for h in range(d // HEAD_DIM):
            out = out + jnp.where(c // HEAD_DIM == h, acc_ref[h * tq:(h + 1) * tq, :], 0.0)
        o_ref[0] = out


def _attn_sample(q, k_new, v_new, cache_k, cache_v, page_table, bias_rows, nseq, tq, *, pps=4):
    n, d = q.shape
    n_pages = page_table.shape[1]
    groups = n_pages // pps
    rows = (d // HEAD_DIM) * tq
    ck = cache_k.reshape(cache_k.shape[0], PAGE_SIZE, d)
    cv = cache_v.reshape(cache_v.shape[0], PAGE_SIZE, d)
    seq_spec = pl.BlockSpec((1, tq, d), lambda b, g, pt: (b, 0, 0))

    def page_spec(i):
        return pl.BlockSpec(
            (1, PAGE_SIZE, d),
            lambda b, g, pt: (pt[b * n_pages + (groups - 1 - g) * pps + i], 0, 0))

    grid_spec = pltpu.PrefetchScalarGridSpec(
        num_scalar_prefetch=1, grid=(nseq, groups),
        in_specs=[pl.BlockSpec((rows, LANES), lambda b, g, pt: (0, 0)), seq_spec, seq_spec, seq_spec]
        + [page_spec(i) for i in range(pps)] * 2,
        out_specs=seq_spec,
        scratch_shapes=[pltpu.VMEM((rows, d), BF16), pltpu.VMEM((PAGE_SIZE, d), F32),
                        pltpu.VMEM((PAGE_SIZE, d), F32), pltpu.VMEM((rows, LANES), F32),
                        pltpu.VMEM((rows, d), F32)],
    )
    out = pl.pallas_call(
        functools.partial(_attn_sample_kernel, pps=pps),
        out_shape=jax.ShapeDtypeStruct((nseq, tq, d), F32), grid_spec=grid_spec,
        compiler_params=_params(("parallel", "arbitrary")), name="attn_sample",
    )(page_table.reshape(-1), bias_rows, q.reshape(nseq, tq, d), k_new.reshape(nseq, tq, d),
      v_new.reshape(nseq, tq, d), *([ck] * pps), *([cv] * pps))
    return out.reshape(n, d)


def _top16(s, want_rank):
    vals = []
    rank = jnp.full(s.shape, float(PEER_TOPK), F32)
    for r in range(PEER_TOPK):
        m = jnp.max(s, axis=0, keepdims=True)
        hit = s >= m
        if want_rank:
            rank = jnp.where(hit, float(r), rank)
        s = jnp.where(hit, NEG_INF, s)
        vals.append(m)
    return vals, rank


def _max_all(cands):
    cur = list(cands)
    while len(cur) > 1:
        cur = [jnp.maximum(cur[i], cur[i + 1]) for i in range(0, len(cur) - 1, 2)] + (
            [cur[-1]] if len(cur) % 2 else [])
    return jnp.max(cur[0], axis=0, keepdims=True)


def _select_head(s1, s2, top_ref):
    a_vals, rank1 = _top16(s1, True)
    b_vals, _ = _top16(s2, False)
    s1 = s1 - a_vals[0]
    s2 = s2 - b_vals[0]
    for r in range(PEER_TOPK):
        top_ref[r:r + 1, :] = b_vals[r] - b_vals[0]
    b_sh = top_ref[...]
    a_sh = [a_vals[r] - a_vals[0] for r in range(PEER_TOPK)]
    cand0 = [a_sh[r] + b_sh for r in range(PEER_TOPK)]
    cand = list(cand0)
    for _ in range(PEER_TOPK - 1):
        m = _max_all(cand)
        cand = [jnp.where(c >= m, NEG_INF, c) for c in cand]
    tau = _max_all(cand)
    zsum = jnp.zeros_like(tau)
    for c in cand0:
        zsum = zsum + jnp.sum(jnp.where(c >= tau, jnp.exp(c), 0.0), axis=0, keepdims=True)
    m2 = jnp.zeros_like(s2)
    for r in range(PEER_TOPK):
        m2 = m2 + jnp.where(s2 + a_sh[r] >= tau, 1.0, 0.0)
    coef = jnp.exp(s1) / zsum
    return rank1, coef, m2, jnp.exp(s2)


def _peer_select_kernel(x_ref, g_ref, sc_ref, sh_ref, wq_ref, keys_ref,
                        h_ref, rank_ref, coef_ref, m2_ref, e2_ref, s_ref, top_ref):
    ts = x_ref.shape[0]
    h2 = (_rms(x_ref[...], g_ref[...]) * (1.0 + sc_ref[0]) + sh_ref[0]).astype(BF16)
    h_ref[...] = h2
    q_t = _dot_nt(wq_ref[...], h2).astype(BF16)
    for blk in range(2 * PEER_HEADS):
        s_ref[blk] = _dot(keys_ref[blk], q_t[blk * PEER_NKEYS:(blk + 1) * PEER_NKEYS, :])

    def head(h, _):
        for c in range(ts // LANES):
            cs = slice(c * LANES, (c + 1) * LANES)
            rank1, coef, m2, e2 = _select_head(s_ref[2 * h, :, cs], s_ref[2 * h + 1, :, cs], top_ref)
            rank_ref[h, :, cs] = rank1
            coef_ref[h, :, cs] = coef
            m2_ref[h, :, cs] = m2
            e2_ref[h, :, cs] = e2
        return 0

    lax.fori_loop(0, PEER_HEADS, head, 0)


def _peer_select(x, g, wq_t, keys, group, *, ts=256):
    n, d = x.shape
    nq = wq_t.shape[0]
    sel = jax.ShapeDtypeStruct((PEER_HEADS, PEER_NKEYS, n), F32)
    sel_spec = pl.BlockSpec((PEER_HEADS, PEER_NKEYS, ts), lambda i: (0, 0, i))
    return pl.pallas_call(
        _peer_select_kernel,
        out_shape=[jax.ShapeDtypeStruct((n, d), BF16), sel, sel, sel, sel],
        grid=(n // ts,),
        in_specs=[
            pl.BlockSpec((ts, d), lambda i: (i, 0)), pl.BlockSpec((1, d), lambda i: (0, 0)),
            group.spec(4, ts, d), group.spec(3, ts, d),
            pl.BlockSpec((nq, d), lambda i: (0, 0)),
            pl.BlockSpec((2 * PEER_HEADS, PEER_NKEYS, PEER_NKEYS), lambda i: (0, 0, 0)),
        ],
        out_specs=[pl.BlockSpec((ts, d), lambda i: (i, 0)), sel_spec, sel_spec, sel_spec, sel_spec],
        scratch_shapes=[pltpu.VMEM((2 * PEER_HEADS, PEER_NKEYS, ts), F32),
                        pltpu.VMEM((PEER_TOPK, LANES), F32)],
        compiler_params=_params(("parallel",)), name="peer_select",
    )(x, g.reshape(1, d), group.mod, group.mod, wq_t, keys)


def _peer_dense_kernel(h_ref, u_ref, vt_ref, rank_ref, coef_ref, m2_ref, e2_ref, x_ref, gate_ref,
                       o_ref, acc_ref, wg_ref):
    e = pl.program_id(1)
    eb = u_ref.shape[0]

    @pl.when(e == 0)
    def _():
        acc_ref[...] = jnp.zeros_like(acc_ref)

    act_t = _dot_nt(u_ref[...], h_ref[...])
    for ii in range(eb // PEER_NKEYS):
        rows = slice(ii * PEER_NKEYS, (ii + 1) * PEER_NKEYS)
        w = jnp.zeros((PEER_NKEYS, h_ref.shape[0]), F32)
        for h in range(PEER_HEADS):
            sel = rank_ref[h, ii:ii + 1, :] < m2_ref[h]
            w = w + jnp.where(sel, e2_ref[h], 0.0) * coef_ref[h, ii:ii + 1, :]
        wg_ref[rows, :] = (w * jax.nn.gelu(act_t[rows, :])).astype(BF16)
    acc_ref[...] += _dot(vt_ref[...], wg_ref[...])

    @pl.when(e == pl.num_programs(1) - 1)
    def _():
        o_ref[...] = x_ref[...] + gate_ref[0] * acc_ref[...].T


def _peer_dense(h2, u, vt, rank1, coef, m2, e2, x, group, *, tt=512, eb=1024):
    n, d = x.shape
    n_exp = u.shape[0]
    ib = eb // PEER_NKEYS
    tile = pl.BlockSpec((tt, d), lambda t, e: (t, 0))
    row_sel = pl.BlockSpec((PEER_HEADS, ib, tt), lambda t, e: (0, e, t))
    all_sel = pl.BlockSpec((PEER_HEADS, PEER_NKEYS, tt), lambda t, e: (0, 0, t))
    if group.per_row:
        gate_spec = pl.BlockSpec((1, tt, d), lambda t, e: (5, t, 0))
    else:
        tiles = group.rows_per_batch // tt
        gate_spec = pl.BlockSpec((1, 1, d), lambda t, e: ((t // tiles) * 6 + 5, 0, 0))
    return pl.pallas_call(
        _peer_dense_kernel,
        out_shape=jax.ShapeDtypeStruct((n, d), F32), grid=(n // tt, n_exp // eb),
        in_specs=[
            tile,
            pl.BlockSpec((eb, d), lambda t, e: (e, 0)),
            pl.BlockSpec((d, eb), lambda t, e: (0, e)),
            row_sel, row_sel, all_sel, all_sel, tile, gate_spec,
        ],
        out_specs=tile,
        scratch_shapes=[pltpu.VMEM((d, tt), F32), pltpu.VMEM((eb, tt), BF16)],
        compiler_params=_params(("parallel", "arbitrary")), name="peer_dense",
    )(h2, u, vt, rank1, coef, m2, e2, x, group.mod)


def _peer(x, g, wq_t, keys, u, vt, group):
    h2, rank1, coef, m2, e2 = _peer_select(x, g, wq_t, keys, group)
    return _peer_dense(h2, u, vt, rank1, coef, m2, e2, x, group)


def _final_norm_kernel(x_ref, g_ref, o_ref):
    o_ref[...] = _rms(x_ref[...], g_ref[...])


def _final_norm(x, g, *, tm=256):
    n, d = x.shape
    tile = pl.BlockSpec((tm, d), lambda i: (i, 0))
    return pl.pallas_call(
        _final_norm_kernel, out_shape=jax.ShapeDtypeStruct((n, d), F32), grid=(n // tm,),
        in_specs=[tile, pl.BlockSpec((1, d), lambda i: (0, 0))], out_specs=tile,
        compiler_params=_params(("parallel",)), name="final_norm",
    )(x, g.reshape(1, d))


def _trunk(x3, groups, conv_hist, past, w, n_conv):
    batch, seq, d = x3.shape
    x = x3.reshape(batch * seq, d)
    glus = []
    k_new = v_new = kb = vb = None
    for l, lp in enumerate(w["layers"]):
        group = groups[l]
        if l < n_conv:
            glu = _nml(x, lp["norm_mix"], lp["w_pw1"], group=group, mod_k=(1, 0), bias=lp["b_pw1"],
                       epi="glu", name="conv_in")
            glus.append(glu)
            if conv_hist is None:
                x = _conv_prompt(glu, x, lp, group, batch, seq)
            else:
                x = _conv_sample(glu, conv_hist[l], x, lp, group, batch, seq)
        else:
            if l == n_conv:
                k_new, v_new, kb, vb = _nml(x, w["norm_kv"], w["w_kv"], epi="kv", name="kv_proj")
            q = _nml(x, lp["norm_mix"], lp["w_q"], group=group, mod_k=(1, 0), epi="scale",
                     scale=HEAD_DIM ** -0.5, name="q_proj")
            if past is None:
                att = _attn_prompt(q, kb, vb, lp["b_sb"], batch, seq)
            else:
                att = _attn_sample(q, k_new, v_new, past["cache_k"], past["cache_v"], past["page_table"],
                                   lp["bias_rows"], batch, seq)
            x = _gated_linear(att, lp["w_o"], x, group, 2, name="attn_out")
        x = _peer(x, lp["norm_ffn"], lp["w_pq_t"], lp["keys"], lp["u"], lp["vt"], group)
    y = _final_norm(x, w["final_norm"])
    return y.reshape(batch, seq, d), glus, k_new, v_new


def kernel(x_prompt, x_sample, cache_k, cache_v, state_conv, page_table, c_prompt, c_sample, w_ada, b_ada, norm_mix, norm_ffn, w_pw1, b_pw1, w_dw, b_dw, ln_g, ln_b, w_pw2, b_pw2, norm_kv, w_k, w_v, w_q, b_sb, w_o, w_pq, sub_keys, expert_u, expert_v, final_norm):
    batch, seq, d = x_prompt.shape
    nseq, tq, _ = x_sample.shape
    depth = w_ada.shape[0]
    n_conv = w_pw1.shape[0]
    heads = d // HEAD_DIM
    nh = CONV_WIDTH - 1

    rows = batch + nseq
    rows_pad = -(-rows // 8) * 8
    c_all = jnp.concatenate([c_prompt, c_sample, jnp.zeros((rows_pad - rows, d), F32)], axis=0)
    mod = _ada(c_all, w_ada, b_ada)

    def groups(l):
        mp = mod[l, :batch].reshape(batch * 6, 1, d)
        ms = jnp.repeat(mod[l, batch:rows].reshape(nseq, 6, d), tq, axis=0).transpose(1, 0, 2)
        return _Group(mp, False, seq), _Group(ms, True, tq)

    layers = []
    for l in range(depth):
        lp = {
            "norm_mix": norm_mix[l], "norm_ffn": norm_ffn[l],
            "w_pq_t": w_pq[l].T.astype(BF16),
            "keys": sub_keys[l].reshape(2 * PEER_HEADS, PEER_NKEYS, -1).astype(BF16),
            "u": expert_u[l].astype(BF16),
            "vt": expert_v[l].T.astype(BF16),
        }
        if l < n_conv:
            lp.update({
                "w_pw1": w_pw1[l].astype(BF16), "b_pw1": b_pw1[l],
                "w_dw": jnp.pad(w_dw[l], ((0, HALO - CONV_WIDTH), (0, 0))),
                "b_dw": b_dw[l].reshape(1, d), "ln_g": ln_g[l].reshape(1, d), "ln_b": ln_b[l].reshape(1, d),
                "w_pw2": w_pw2[l].astype(BF16), "b_pw2": b_pw2[l].reshape(1, d),
            })
        else:
            j = l - n_conv
            lp.update({
                "w_q": w_q[j].astype(BF16), "w_o": w_o[j].astype(BF16), "b_sb": b_sb[j],
                "bias_rows": jnp.broadcast_to(jnp.repeat(b_sb[j], tq)[:, None], (heads * tq, LANES)),
            })
        layers.append(lp)
    w = {"norm_kv": norm_kv, "w_kv": jnp.concatenate([w_k, w_v], axis=1).astype(BF16),
         "final_norm": final_norm, "layers": layers}
    per_layer = [groups(l) for l in range(depth)]

    y_p, glu_p, k_p, v_p = _trunk(x_prompt, [g[0] for g in per_layer], None, None, w, n_conv)
    past = {"cache_k": cache_k, "cache_v": cache_v, "page_table": page_table}
    y_s, glu_s, k_s, v_s = _trunk(x_sample, [g[1] for g in per_layer], state_conv, past, w, n_conv)

    conv_prompt = jnp.stack([g.reshape(batch, seq, d)[:, seq - nh:] for g in glu_p])
    conv_sample = jnp.stack([
        jnp.concatenate([state_conv[l], g.reshape(nseq, tq, d)], axis=1)[:, tq:] for l, g in enumerate(glu_s)])
    shape_p = (batch, seq, heads, HEAD_DIM)
    shape_s = (nseq, tq, heads, HEAD_DIM)
    return (y_p, y_s, conv_prompt, conv_sample, k_p.reshape(shape_p), v_p.reshape(shape_p),
            k_s.reshape(shape_s), v_s.reshape(shape_s))
```

```python
import functools

import jax
import jax.numpy as jnp
from jax import lax
from jax.experimental import pallas as pl
from jax.experimental.pallas import tpu as pltpu

F32 = jnp.float32
BF16 = jnp.bfloat16

EPS = 1e-6
HEAD_DIM = 64
CONV_WIDTH = 31
PAGE_SIZE = 128
PEER_HEADS = 8
PEER_TOPK = 16
PEER_NKEYS = 128
LANES = 128
BF16_ROWS = 16
MXU_DEPTH = 256
HALO = 32
VMEM_LIMIT = 56 * 1024 * 1024
NEG_INF = float("-inf")
LOG2E = 1.4426950408889634


def _params(sem, vmem=VMEM_LIMIT, flags=None):
    return pltpu.CompilerParams(dimension_semantics=sem, vmem_limit_bytes=vmem, flags=flags)


def _dot(a, b):
    return jnp.dot(a, b, preferred_element_type=F32)


def _dot_nt(a, b):
    return lax.dot_general(a, b, (((1,), (1,)), ((), ())), preferred_element_type=F32)


def _rms(x, g):
    return x * lax.rsqrt(jnp.mean(x * x, axis=-1, keepdims=True) + EPS) * g


def _softplus2(z2):
    return jnp.maximum(z2, 0.0) + jnp.log2(1.0 + jnp.exp2(-jnp.abs(z2)))


def _gelu_tanh(x):
    k = -2.0 * (2.0 / jnp.pi) ** 0.5 * LOG2E
    return x / (1.0 + jnp.exp2(x * (x * x * (k * 0.044715) + k)))


def _split_bf16(x):
    hi = x.astype(BF16)
    lo = (x - hi.astype(F32)).astype(BF16)
    return hi, lo


def _ada_kernel(c_ref, w_ref, b_ref, o_ref):
    c = c_ref[...]
    s = (c * jax.nn.sigmoid(c)).astype(BF16)
    o_ref[0] = _dot(s, w_ref[0].astype(BF16)) + b_ref[0]


def _ada(c_all, w_ada, b_ada):
    depth, d, n6 = w_ada.shape
    rows = c_all.shape[0]
    tn = 1024
    return pl.pallas_call(
        _ada_kernel,
        out_shape=jax.ShapeDtypeStruct((depth, rows, n6), F32),
        grid=(depth, n6 // tn),
        in_specs=[
            pl.BlockSpec((rows, d), lambda l, j: (0, 0)),
            pl.BlockSpec((1, d, tn), lambda l, j: (l, 0, j)),
            pl.BlockSpec((1, 1, tn), lambda l, j: (l, 0, j)),
        ],
        out_specs=pl.BlockSpec((1, rows, tn), lambda l, j: (l, 0, j)),
        compiler_params=_params(("parallel", "parallel")),
        name="ada",
    )(c_all, w_ada, b_ada.reshape(depth, 1, n6))


class _Group:
    def __init__(self, mod, per_row, rows_per_batch):
        self.mod = mod
        self.per_row = per_row
        self.rows_per_batch = rows_per_batch

    def spec(self, k, tm, d):
        if self.per_row:
            return pl.BlockSpec((1, tm, d), lambda i, *_: (k, i, 0))
        tiles = self.rows_per_batch // tm
        return pl.BlockSpec((1, 1, d), lambda i, *_: ((i // tiles) * 6 + k, 0, 0))


def _nml_kernel(*refs, has_mod, has_bias, epi, scale):
    it = iter(refs)
    x_ref, g_ref = next(it), next(it)
    sc_ref = sh_ref = b_ref = None
    if has_mod:
        sc_ref, sh_ref = next(it), next(it)
    w_ref = next(it)
    if has_bias:
        b_ref = next(it)
    outs = list(it)
    y = _rms(x_ref[...], g_ref[...])
    if has_mod:
        y = y * (1.0 + sc_ref[0]) + sh_ref[0]
    a = _dot(y.astype(BF16), w_ref[...])
    if has_bias:
        a = a + b_ref[...]
    d = x_ref.shape[1]
    if epi == "glu":
        outs[0][...] = a[:, :d] * jax.nn.sigmoid(a[:, d:])
    elif epi == "scale":
        outs[0][...] = a * scale
    else:
        k, v = a[:, :d], a[:, d:]
        outs[0][...] = k
        outs[1][...] = v
        outs[2][...] = k.astype(BF16)
        outs[3][...] = v.astype(BF16)


def _nml(x, g, w_bf16, *, group=None, mod_k=None, bias=None, epi, scale=1.0, tm=256, name):
    n, d = x.shape
    nout = w_bf16.shape[1]
    args = [x, g.reshape(1, d)]
    specs = [pl.BlockSpec((tm, d), lambda i: (i, 0)), pl.BlockSpec((1, d), lambda i: (0, 0))]
    if group is not None:
        args += [group.mod, group.mod]
        specs += [group.spec(mod_k[0], tm, d), group.spec(mod_k[1], tm, d)]
    args.append(w_bf16)
    specs.append(pl.BlockSpec((d, nout), lambda i: (0, 0)))
    if bias is not None:
        args.append(bias.reshape(1, nout))
        specs.append(pl.BlockSpec((1, nout), lambda i: (0, 0)))
    tile = pl.BlockSpec((tm, d), lambda i: (i, 0))
    if epi == "kv":
        out_shape = [jax.ShapeDtypeStruct((n, d), F32)] * 2 + [jax.ShapeDtypeStruct((n, d), BF16)] * 2
        out_specs = [tile] * 4
    else:
        out_shape = jax.ShapeDtypeStruct((n, d), F32)
        out_specs = tile
    return pl.pallas_call(
        functools.partial(_nml_kernel, has_mod=group is not None, has_bias=bias is not None,
                          epi=epi, scale=scale),
        out_shape=out_shape, grid=(n // tm,), in_specs=specs, out_specs=out_specs,
        compiler_params=_params(("parallel",)), name=name,
    )(*args)


def _gated_linear_kernel(a_ref, w_ref, x_ref, gate_ref, o_ref):
    o_ref[...] = x_ref[...] + gate_ref[0] * _dot(a_ref[...].astype(BF16), w_ref[...])


def _gated_linear(a, w_bf16, x, group, gate_k, *, tm=256, name):
    n, d = x.shape
    tile = pl.BlockSpec((tm, d), lambda i: (i, 0))
    return pl.pallas_call(
        _gated_linear_kernel,
        out_shape=jax.ShapeDtypeStruct((n, d), F32), grid=(n // tm,),
        in_specs=[tile, pl.BlockSpec((d, d), lambda i: (0, 0)), tile, group.spec(gate_k, tm, d)],
        out_specs=tile, compiler_params=_params(("parallel",)), name=name,
    )(a, w_bf16, x, group.mod)


def _conv_tail(y, lng_ref, lnb_ref, w2_ref, b2_ref, x_ref, gate_ref, o_ref):
    mu = jnp.mean(y, axis=-1, keepdims=True)
    yc = y - mu
    var = jnp.mean(yc * yc, axis=-1, keepdims=True)
    yn = yc * lax.rsqrt(var + EPS) * lng_ref[...] + lnb_ref[...]
    act = yn * jax.nn.sigmoid(yn)
    out = _dot(act.astype(BF16), w2_ref[...]) + b2_ref[...]
    o_ref[...] = x_ref[...] + gate_ref[0] * out


def _conv_prompt_kernel(glu_ref, halo_ref, wdw_ref, bdw_ref, lng_ref, lnb_ref, w2_ref, b2_ref,
                        x_ref, gate_ref, o_ref, buf_ref, y_ref):
    t = pl.program_id(1)
    tm, d = x_ref.shape
    halo = halo_ref[0]
    buf_ref[0:HALO, :] = jnp.where(t > 0, halo, jnp.zeros_like(halo))
    buf_ref[HALO:HALO + tm, :] = glu_ref[0]
    first = HALO - (CONV_WIDTH - 1)
    for c in range(d // LANES):
        cs = slice(c * LANES, (c + 1) * LANES)
        acc = jnp.broadcast_to(bdw_ref[:, cs], (tm, LANES))
        for k in range(CONV_WIDTH):
            acc = acc + wdw_ref[k:k + 1, cs] * buf_ref[first + k:first + k + tm, cs]
        y_ref[:, cs] = acc
    _conv_tail(y_ref[...], lng_ref, lnb_ref, w2_ref, b2_ref, x_ref, gate_ref, o_ref)


def _conv_prompt(glu, x, lp, group, batch, seq, *, tm=256):
    n, d = x.shape
    tiles = seq // tm
    hb = tm // HALO
    glu3 = glu.reshape(batch, seq, d)
    row = lambda b, t: (0, 0)
    return pl.pallas_call(
        _conv_prompt_kernel,
        out_shape=jax.ShapeDtypeStruct((n, d), F32), grid=(batch, tiles),
        in_specs=[
            pl.BlockSpec((1, tm, d), lambda b, t: (b, t, 0)),
            pl.BlockSpec((1, HALO, d), lambda b, t: (b, jnp.maximum(t * hb - 1, 0), 0)),
            pl.BlockSpec((HALO, d), row), pl.BlockSpec((1, d), row),
            pl.BlockSpec((1, d), row), pl.BlockSpec((1, d), row),
            pl.BlockSpec((d, d), row), pl.BlockSpec((1, d), row),
            pl.BlockSpec((tm, d), lambda b, t: (b * tiles + t, 0)),
            pl.BlockSpec((1, 1, d), lambda b, t: (b * 6 + 2, 0, 0)),
        ],
        out_specs=pl.BlockSpec((tm, d), lambda b, t: (b * tiles + t, 0)),
        scratch_shapes=[pltpu.VMEM((HALO + tm, d), F32), pltpu.VMEM((tm, d), F32)],
        compiler_params=_params(("parallel", "arbitrary")), name="conv_prompt",
    )(glu3, glu3, lp["w_dw"], lp["b_dw"], lp["ln_g"], lp["ln_b"], lp["w_pw2"], lp["b_pw2"], x, group.mod)


def _conv_sample_kernel(hist_ref, glu_ref, wdw_ref, bdw_ref, lng_ref, lnb_ref, w2_ref, b2_ref,
                        x_ref, gate_ref, o_ref, buf_ref):
    sb, tq, d = glu_ref.shape
    nh = CONV_WIDTH - 1
    buf_ref[:, 0:nh, :] = hist_ref[...]
    buf_ref[:, nh:nh + tq, :] = glu_ref[...]
    acc = jnp.broadcast_to(bdw_ref[...].reshape(1, 1, d), (sb, tq, d))
    for k in range(CONV_WIDTH):
        acc = acc + wdw_ref[k:k + 1, :].reshape(1, 1, d) * buf_ref[:, k:k + tq, :]
    _conv_tail(acc.reshape(sb * tq, d), lng_ref, lnb_ref, w2_ref, b2_ref, x_ref, gate_ref, o_ref)


def _conv_sample(glu, hist, x, lp, group, nseq, tq, *, sb=16):
    n, d = x.shape
    nh = CONV_WIDTH - 1
    row = lambda i: (0, 0)
    return pl.pallas_call(
        _conv_sample_kernel,
        out_shape=jax.ShapeDtypeStruct((n, d), F32), grid=(nseq // sb,),
        in_specs=[
            pl.BlockSpec((sb, nh, d), lambda i: (i, 0, 0)),
            pl.BlockSpec((sb, tq, d), lambda i: (i, 0, 0)),
            pl.BlockSpec((HALO, d), row), pl.BlockSpec((1, d), row),
            pl.BlockSpec((1, d), row), pl.BlockSpec((1, d), row),
            pl.BlockSpec((d, d), row), pl.BlockSpec((1, d), row),
            pl.BlockSpec((sb * tq, d), lambda i: (i, 0)),
            pl.BlockSpec((1, sb * tq, d), lambda i: (2, i, 0)),
        ],
        out_specs=pl.BlockSpec((sb * tq, d), lambda i: (i, 0)),
        scratch_shapes=[pltpu.VMEM((sb, nh + tq + 2, d), F32)],
        compiler_params=_params(("parallel",)), name="conv_sample",
    )(hist, glu.reshape(nseq, tq, d), lp["w_dw"], lp["b_dw"], lp["ln_g"], lp["ln_b"],
      lp["w_pw2"], lp["b_pw2"], x, group.mod)


def _suffix_matrix(n):
    j = lax.broadcasted_iota(jnp.int32, (n, n), 0)
    s = lax.broadcasted_iota(jnp.int32, (n, n), 1)
    return jnp.where(j > s, 1.0, 0.0).astype(BF16)


def _sb_block(z2, mask, carry, m_suffix):
    rows = z2.shape[0]
    sp = _softplus2(z2)
    if mask is not None:
        sp = jnp.where(mask, sp, 0.0)
    hi, lo = _split_bf16(sp)
    both = _dot(jnp.concatenate([hi, lo], axis=0), m_suffix)
    after = both[:rows] + both[rows:]
    reps = z2.shape[1] // LANES
    carry_full = carry if reps == 1 else jnp.concatenate([carry] * reps, axis=1)
    a = jnp.exp2(z2 - sp - after - carry_full)
    if mask is not None:
        a = jnp.where(mask, a, 0.0)
    row_sum = after[:, 0:1] + sp[:, 0:1]
    return a, carry + jnp.broadcast_to(row_sum, carry.shape)


def _attn_prompt_kernel(bias_ref, q_ref, k_ref, v_ref, o_ref, *, kbs):
    hp = pl.program_id(1)
    qi = pl.program_id(2)
    tq = q_ref.shape[1]
    nkb = tq // kbs
    q2 = q_ref[0]
    lane = lax.broadcasted_iota(jnp.int32, (tq, LANES), 1)
    qs = jnp.concatenate([jnp.where(lane < HEAD_DIM, q2, 0.0), jnp.where(lane >= HEAD_DIM, q2, 0.0)],
                         axis=0).astype(BF16)
    row = lax.broadcasted_iota(jnp.int32, (2 * tq, kbs), 0)
    col = lax.broadcasted_iota(jnp.int32, (2 * tq, kbs), 1)
    bias = jnp.where(row < tq, bias_ref[2 * hp], bias_ref[2 * hp + 1])
    qrow = jnp.where(row < tq, row, row - tq)
    m_suffix = _suffix_matrix(kbs)

    def block(kb, mask, carry, acc):
        ks = pl.multiple_of(kb * kbs, kbs)
        z2 = _dot_nt(qs, k_ref[0, pl.ds(ks, kbs), :]) + bias
        a, carry = _sb_block(z2, mask, carry, m_suffix)
        return carry, acc + _dot(a.astype(BF16), v_ref[0, pl.ds(ks, kbs), :])

    carry = jnp.zeros((2 * tq, LANES), F32)
    acc = jnp.zeros((2 * tq, LANES), F32)
    for j in reversed(range(nkb)):
        carry, acc = block(qi * nkb + j, col + j * kbs < qrow, carry, acc)

    def body(step, c):
        kb = qi * nkb - 1 - 2 * step
        c = block(kb, None, *c)
        return block(kb - 1, None, *c)

    carry, acc = lax.fori_loop(0, qi * (nkb // 2), body, (carry, acc))
    o_ref[0] = jnp.where(lane < HEAD_DIM, acc[:tq], acc[tq:])


def _attn_prompt(q, kb, vb, bias, batch, seq, *, tq=512, kbs=256):
    n, d = q.shape
    tq = min(tq, seq)
    assert tq % (2 * kbs) == 0 and seq % tq == 0
    q3, k3, v3 = (a.reshape(batch, seq, d) for a in (q, kb, vb))
    hpairs = d // LANES
    out = pl.pallas_call(
        functools.partial(_attn_prompt_kernel, kbs=kbs),
        out_shape=jax.ShapeDtypeStruct((batch, seq, d), F32),
        grid=(batch, hpairs, seq // tq),
        in_specs=[
            pl.BlockSpec(memory_space=pltpu.SMEM),
            pl.BlockSpec((1, tq, LANES), lambda b, h, i: (b, i, h)),
            pl.BlockSpec((1, seq, LANES), lambda b, h, i: (b, 0, h)),
            pl.BlockSpec((1, seq, LANES), lambda b, h, i: (b, 0, h)),
        ],
        out_specs=pl.BlockSpec((1, tq, LANES), lambda b, h, i: (b, i, h)),
        compiler_params=_params(("parallel", "parallel", "arbitrary")), name="attn_prompt",
    )(bias, q3, k3, v3)
    return out.reshape(n, d)


def _attn_sample_kernel(pt_ref, bias_ref, q_ref, kn_ref, vn_ref, *rest, pps):
    kp_refs, vp_refs = rest[:pps], rest[pps:2 * pps]
    o_ref, qbd_ref, knp_ref, vnp_ref, carry_ref, acc_ref = rest[2 * pps:]
    g = pl.program_id(1)
    tq, d = q_ref.shape[1], q_ref.shape[2]
    rows = (d // HEAD_DIM) * tq
    m_suffix = _suffix_matrix(PAGE_SIZE)
    bias = bias_ref[...]

    @pl.when(g == 0)
    def _():
        r = lax.broadcasted_iota(jnp.int32, (rows, d), 0)
        c = lax.broadcasted_iota(jnp.int32, (rows, d), 1)
        qt = jnp.concatenate([q_ref[0]] * (rows // tq), axis=0)
        qbd_ref[...] = jnp.where(r // tq == c // HEAD_DIM, qt, 0.0).astype(BF16)
        knp_ref[...] = jnp.zeros_like(knp_ref)
        vnp_ref[...] = jnp.zeros_like(vnp_ref)
        knp_ref[0:tq, :] = kn_ref[0]
        vnp_ref[0:tq, :] = vn_ref[0]
        z = _dot_nt(qbd_ref[...], knp_ref[...].astype(BF16)) + bias
        rr = lax.broadcasted_iota(jnp.int32, (rows, PAGE_SIZE), 0)
        cc = lax.broadcasted_iota(jnp.int32, (rows, PAGE_SIZE), 1)
        mask = cc < rr % tq
        a, carry = _sb_block(z, mask, jnp.zeros((rows, LANES), F32), m_suffix)
        carry_ref[...] = carry
        acc_ref[...] = _dot(a.astype(BF16), vnp_ref[...].astype(BF16))

    qbd = qbd_ref[...]
    for i in reversed(range(pps)):
        z = _dot_nt(qbd, kp_refs[i][0].astype(BF16)) + bias
        a, carry = _sb_block(z, None, carry_ref[...], m_suffix)
        carry_ref[...] = carry
        acc_ref[...] += _dot(a.astype(BF16), vp_refs[i][0].astype(BF16))

    @pl.when(g == pl.num_programs(1) - 1)
    def _():
        c = lax.broadcasted_iota(jnp.int32, (tq, d), 1)
        out = jnp.zeros((tq, d), F32)
        for h in range(d // HEAD_DIM):
            out = out + jnp.where(c // HEAD_DIM == h, acc_ref[h * tq:(h + 1) * tq, :], 0.0)
        o_ref[0] = out


def _attn_sample(q, k_new, v_new, cache_k, cache_v, page_table, bias_rows, nseq, tq, *, pps=4):
    n, d = q.shape
    n_pages = page_table.shape[1]
    groups = n_pages // pps
    rows = (d // HEAD_DIM) * tq
    ck = cache_k.reshape(cache_k.shape[0], PAGE_SIZE, d)
    cv = cache_v.reshape(cache_v.shape[0], PAGE_SIZE, d)
    seq_spec = pl.BlockSpec((1, tq, d), lambda b, g, pt: (b, 0, 0))

    def page_spec(i):
        return pl.BlockSpec(
            (1, PAGE_SIZE, d),
            lambda b, g, pt: (pt[b * n_pages + (groups - 1 - g) * pps + i], 0, 0))

    grid_spec = pltpu.PrefetchScalarGridSpec(
        num_scalar_prefetch=1, grid=(nseq, groups),
        in_specs=[pl.BlockSpec((rows, LANES), lambda b, g, pt: (0, 0)), seq_spec, seq_spec, seq_spec]
        + [page_spec(i) for i in range(pps)] * 2,
        out_specs=seq_spec,
        scratch_shapes=[pltpu.VMEM((rows, d), BF16), pltpu.VMEM((PAGE_SIZE, d), F32),
                        pltpu.VMEM((PAGE_SIZE, d), F32), pltpu.VMEM((rows, LANES), F32),
                        pltpu.VMEM((rows, d), F32)],
    )
    out = pl.pallas_call(
        functools.partial(_attn_sample_kernel, pps=pps),
        out_shape=jax.ShapeDtypeStruct((nseq, tq, d), F32), grid_spec=grid_spec,
        compiler_params=_params(("parallel", "arbitrary")), name="attn_sample",
    )(page_table.reshape(-1), bias_rows, q.reshape(nseq, tq, d), k_new.reshape(nseq, tq, d),
      v_new.reshape(nseq, tq, d), *([ck] * pps), *([cv] * pps))
    return out.reshape(n, d)


def _top16(s, want_rank):
    vals = []
    rank = jnp.full(s.shape, float(PEER_TOPK), F32)
    for r in range(PEER_TOPK):
        m = jnp.max(s, axis=0, keepdims=True)
        hit = s >= m
        if want_rank:
            rank = jnp.where(hit, float(r), rank)
        s = jnp.where(hit, NEG_INF, s)
        vals.append(m)
    return vals, rank


def _max_all(cands):
    cur = list(cands)
    while len(cur) > 1:
        cur = [jnp.maximum(cur[i], cur[i + 1]) for i in range(0, len(cur) - 1, 2)] + (
            [cur[-1]] if len(cur) % 2 else [])
    return jnp.max(cur[0], axis=0, keepdims=True)


def _select_head(s1, s2, top_ref):
    a_vals, rank1 = _top16(s1, True)
    b_vals, _ = _top16(s2, False)
    s1 = s1 - a_vals[0]
    s2 = s2 - b_vals[0]
    for r in range(PEER_TOPK):
        top_ref[0, r:r + 1, :] = a_vals[r] - a_vals[0]
        top_ref[1, r:r + 1, :] = b_vals[r] - b_vals[0]
    a_sh, b_sh = top_ref[0], top_ref[1]
    half = PEER_TOPK // 2
    sub = lax.broadcasted_iota(jnp.int32, (half, s1.shape[1]), 0)
    cand0 = [b_sh[0:half], b_sh[half:]]
    for r in range(1, half):
        cand0.append(jnp.where(sub < PEER_TOPK // (r + 1), a_sh[r:r + 1] + b_sh[0:half], NEG_INF))
    cand0.append(a_sh[half:])
    cand = list(cand0)
    for _ in range(PEER_TOPK - 1):
        m = _max_all(cand)
        cand = [jnp.where(c >= m, NEG_INF, c) for c in cand]
    tau = _max_all(cand)
    zsum = jnp.zeros_like(tau)
    for c in cand0:
        zsum = zsum + jnp.sum(jnp.where(c >= tau, jnp.exp(c), 0.0), axis=0, keepdims=True)
    m2 = jnp.where(s2 >= tau, 1.0, 0.0)
    for r in range(1, half):
        m2 = m2 + jnp.where(s2 + a_sh[r:r + 1] >= tau, 1.0, 0.0)
    deep = jnp.sum(jnp.where(a_sh[half:] >= tau, 1.0, 0.0), axis=0, keepdims=True)
    m2 = m2 + jnp.where(s2 >= 0.0, deep, 0.0)
    coef = jnp.exp(s1) / zsum
    return rank1, coef, m2.astype(BF16), jnp.exp(s2).astype(BF16)


def _peer_select_kernel(x_ref, g_ref, sc_ref, sh_ref, wq_ref, keys_ref,
                        h_ref, rank_ref, coef_ref, m2_ref, e2_ref, s_ref, top_ref):
    ts = x_ref.shape[0]
    h2 = _rms(x_ref[...], g_ref[...]) * (1.0 + sc_ref[0]) + sh_ref[0]
    h_t = h2.T.astype(BF16)
    h_ref[...] = h_t
    q_t = _dot(wq_ref[...], h_t).astype(BF16)
    for blk in range(2 * PEER_HEADS):
        s_ref[blk] = _dot(keys_ref[blk], q_t[blk * PEER_NKEYS:(blk + 1) * PEER_NKEYS, :])

    def head(h, _):
        for c in range(ts // LANES):
            cs = slice(c * LANES, (c + 1) * LANES)
            rank1, coef, m2, e2 = _select_head(s_ref[2 * h, :, cs], s_ref[2 * h + 1, :, cs], top_ref)
            rank_ref[h, :, cs] = rank1
            coef_ref[h, :, cs] = coef
            m2_ref[h, :, cs] = m2
            e2_ref[h, :, cs] = e2
        return 0

    lax.fori_loop(0, PEER_HEADS, head, 0)


def _peer_select(x, g, wq_t, keys, group, *, ts=256):
    n, d = x.shape
    nq = wq_t.shape[0]
    sel = jax.ShapeDtypeStruct((PEER_HEADS, PEER_NKEYS, n), F32)
    sel_b = jax.ShapeDtypeStruct((PEER_HEADS, PEER_NKEYS, n), BF16)
    sel_spec = pl.BlockSpec((PEER_HEADS, PEER_NKEYS, ts), lambda i: (0, 0, i))
    return pl.pallas_call(
        _peer_select_kernel,
        out_shape=[jax.ShapeDtypeStruct((d, n), BF16), sel, sel, sel_b, sel_b],
        grid=(n // ts,),
        in_specs=[
            pl.BlockSpec((ts, d), lambda i: (i, 0)), pl.BlockSpec((1, d), lambda i: (0, 0)),
            group.spec(4, ts, d), group.spec(3, ts, d),
            pl.BlockSpec((nq, d), lambda i: (0, 0)),
            pl.BlockSpec((2 * PEER_HEADS, PEER_NKEYS, PEER_NKEYS), lambda i: (0, 0, 0)),
        ],
        out_specs=[pl.BlockSpec((d, ts), lambda i: (0, i)), sel_spec, sel_spec, sel_spec, sel_spec],
        scratch_shapes=[pltpu.VMEM((2 * PEER_HEADS, PEER_NKEYS, ts), F32),
                        pltpu.VMEM((2, PEER_TOPK, LANES), F32)],
        compiler_params=_params(("parallel",)), name="peer_select",
    )(x, g.reshape(1, d), group.mod, group.mod, wq_t, keys)


def _peer_dense_kernel(h_ref, u0_ref, un_ref, vt_ref, rank_ref, coef_ref, m2_ref, e2_ref, x_ref, gate_ref,
                       o_ref, acc_ref, act_a, act_b):
    e = pl.program_id(1)
    eb = un_ref.shape[0]
    tt = h_ref.shape[1]

    @pl.when(e == 0)
    def _():
        acc_ref[...] = jnp.zeros_like(acc_ref)
        act_a[...] = _dot(u0_ref[...], h_ref[...])

    def step(act_cur, act_next):
        act_next[...] = _dot(un_ref[...], h_ref[...])
        parts = []
        for ii in range(eb // PEER_NKEYS):
            rows = slice(ii * PEER_NKEYS, (ii + 1) * PEER_NKEYS)
            w = jnp.zeros(m2_ref.shape[1:], BF16)
            for h in range(PEER_HEADS):
                r16 = jnp.broadcast_to(rank_ref[h, ii:ii + 1, :], (BF16_ROWS, tt)).astype(BF16)
                c16 = jnp.broadcast_to(coef_ref[h, ii:ii + 1, :], (BF16_ROWS, tt)).astype(BF16)
                w = w + jnp.where(r16[None] < m2_ref[h], e2_ref[h], 0) * c16[None]
            g = _gelu_tanh(act_cur[rows, :]).astype(BF16).reshape(w.shape)
            parts.append((w * g).reshape(PEER_NKEYS, tt))
        acc_ref[...] += _dot(vt_ref[...], jnp.concatenate(parts, axis=0))

    @pl.when(e % 2 == 0)
    def _():
        step(act_a, act_b)

    @pl.when(e % 2 == 1)
    def _():
        step(act_b, act_a)

    @pl.when(e == pl.num_programs(1) - 1)
    def _():
        o_ref[...] = x_ref[...] + gate_ref[0] * acc_ref[...].T


def _peer_dense(h2, u, vt, rank1, coef, m2, e2, x, group, *, tt=512, eb=1024):
    n, d = x.shape
    n_exp = u.shape[0]
    ib = eb // PEER_NKEYS
    last = n_exp // eb - 1
    tile = pl.BlockSpec((tt, d), lambda t, e: (t, 0))
    row_sel = pl.BlockSpec((PEER_HEADS, ib, tt), lambda t, e: (0, e, t))
    packed = (PEER_HEADS, PEER_NKEYS // BF16_ROWS, BF16_ROWS, n)
    m2, e2 = m2.reshape(packed), e2.reshape(packed)
    all_sel = pl.BlockSpec(packed[:3] + (tt,), lambda t, e: (0, 0, 0, t))
    if group.per_row:
        gate_spec = pl.BlockSpec((1, tt, d), lambda t, e: (5, t, 0))
    else:
        tiles = group.rows_per_batch // tt
        gate_spec = pl.BlockSpec((1, 1, d), lambda t, e: ((t // tiles) * 6 + 5, 0, 0))
    return pl.pallas_call(
        _peer_dense_kernel,
        out_shape=jax.ShapeDtypeStruct((n, d), F32), grid=(n // tt, n_exp // eb),
        in_specs=[
            pl.BlockSpec((d, tt), lambda t, e: (0, t)),
            pl.BlockSpec((eb, d), lambda t, e: (0, 0)),
            pl.BlockSpec((eb, d), lambda t, e: (jnp.minimum(e + 1, last), 0)),
            pl.BlockSpec((d, eb), lambda t, e: (0, e)),
            row_sel, row_sel, all_sel, all_sel, tile, gate_spec,
        ],
        out_specs=tile,
        scratch_shapes=[pltpu.VMEM((d, tt), F32), pltpu.VMEM((eb, tt), F32), pltpu.VMEM((eb, tt), F32)],
        compiler_params=_params(("parallel", "arbitrary")), name="peer_dense",
    )(h2, u, u, vt, rank1, coef, m2, e2, x, group.mod)


def _peer(x, g, wq_t, keys, u, vt, group):
    h2, rank1, coef, m2, e2 = _peer_select(x, g, wq_t, keys, group)
    return _peer_dense(h2, u, vt, rank1, coef, m2, e2, x, group)


def _final_norm_kernel(x_ref, g_ref, o_ref):
    o_ref[...] = _rms(x_ref[...], g_ref[...])


def _final_norm(x, g, *, tm=256):
    n, d = x.shape
    tile = pl.BlockSpec((tm, d), lambda i: (i, 0))
    return pl.pallas_call(
        _final_norm_kernel, out_shape=jax.ShapeDtypeStruct((n, d), F32), grid=(n // tm,),
        in_specs=[tile, pl.BlockSpec((1, d), lambda i: (0, 0))], out_specs=tile,
        compiler_params=_params(("parallel",)), name="final_norm",
    )(x, g.reshape(1, d))


def _trunk(x3, groups, conv_hist, past, w, n_conv):
    batch, seq, d = x3.shape
    x = x3.reshape(batch * seq, d)
    glus = []
    k_new = v_new = kb = vb = None
    for l, lp in enumerate(w["layers"]):
        group = groups[l]
        if l < n_conv:
            glu = _nml(x, lp["norm_mix"], lp["w_pw1"], group=group, mod_k=(1, 0), bias=lp["b_pw1"],
                       epi="glu", name="conv_in")
            glus.append(glu)
            if conv_hist is None:
                x = _conv_prompt(glu, x, lp, group, batch, seq)
            else:
                x = _conv_sample(glu, conv_hist[l], x, lp, group, batch, seq)
        else:
            if l == n_conv:
                k_new, v_new, kb, vb = _nml(x, w["norm_kv"], w["w_kv"], epi="kv", name="kv_proj")
            q = _nml(x, lp["norm_mix"], lp["w_q"], group=group, mod_k=(1, 0), epi="scale",
                     scale=HEAD_DIM ** -0.5 * LOG2E, name="q_proj")
            if past is None:
                att = _attn_prompt(q, kb, vb, lp["b_sb"], batch, seq)
            else:
                att = _attn_sample(q, k_new, v_new, past["cache_k"], past["cache_v"], past["page_table"],
                                   lp["bias_rows"], batch, seq)
            x = _gated_linear(att, lp["w_o"], x, group, 2, name="attn_out")
        x = _peer(x, lp["norm_ffn"], lp["w_pq_t"], lp["keys"], lp["u"], lp["vt"], group)
    y = _final_norm(x, w["final_norm"])
    return y.reshape(batch, seq, d), glus, k_new, v_new


def kernel(x_prompt, x_sample, cache_k, cache_v, state_conv, page_table, c_prompt, c_sample, w_ada, b_ada, norm_mix, norm_ffn, w_pw1, b_pw1, w_dw, b_dw, ln_g, ln_b, w_pw2, b_pw2, norm_kv, w_k, w_v, w_q, b_sb, w_o, w_pq, sub_keys, expert_u, expert_v, final_norm):
    batch, seq, d = x_prompt.shape
    nseq, tq, _ = x_sample.shape
    depth = w_ada.shape[0]
    n_conv = w_pw1.shape[0]
    heads = d // HEAD_DIM
    nh = CONV_WIDTH - 1

    rows = batch + nseq
    rows_pad = -(-rows // 8) * 8
    c_all = jnp.concatenate([c_prompt, c_sample, jnp.zeros((rows_pad - rows, d), F32)], axis=0)
    mod = _ada(c_all, w_ada, b_ada)

    def groups(l):
        mp = mod[l, :batch].reshape(batch * 6, 1, d)
        ms = jnp.repeat(mod[l, batch:rows].reshape(nseq, 6, d), tq, axis=0).transpose(1, 0, 2)
        return _Group(mp, False, seq), _Group(ms, True, tq)

    layers = []
    for l in range(depth):
        lp = {
            "norm_mix": norm_mix[l], "norm_ffn": norm_ffn[l],
            "w_pq_t": w_pq[l].T.astype(BF16),
            "keys": sub_keys[l].reshape(2 * PEER_HEADS, PEER_NKEYS, -1).astype(BF16),
            "u": expert_u[l].astype(BF16),
            "vt": expert_v[l].T.astype(BF16),
        }
        if l < n_conv:
            lp.update({
                "w_pw1": w_pw1[l].astype(BF16), "b_pw1": b_pw1[l],
                "w_dw": jnp.pad(w_dw[l], ((0, HALO - CONV_WIDTH), (0, 0))),
                "b_dw": b_dw[l].reshape(1, d), "ln_g": ln_g[l].reshape(1, d), "ln_b": ln_b[l].reshape(1, d),
                "w_pw2": w_pw2[l].astype(BF16), "b_pw2": b_pw2[l].reshape(1, d),
            })
        else:
            j = l - n_conv
            lp.update({
                "w_q": w_q[j].astype(BF16), "w_o": w_o[j].astype(BF16), "b_sb": b_sb[j] * LOG2E,
                "bias_rows": jnp.broadcast_to(jnp.repeat(b_sb[j] * LOG2E, tq)[:, None], (heads * tq, LANES)),
            })
        layers.append(lp)
    w = {"norm_kv": norm_kv, "w_kv": jnp.concatenate([w_k, w_v], axis=1).astype(BF16),
         "final_norm": final_norm, "layers": layers}
    per_layer = [groups(l) for l in range(depth)]

    y_p, glu_p, k_p, v_p = _trunk(x_prompt, [g[0] for g in per_layer], None, None, w, n_conv)
    past = {"cache_k": cache_k, "cache_v": cache_v, "page_table": page_table}
    y_s, glu_s, k_s, v_s = _trunk(x_sample, [g[1] for g in per_layer], state_conv, past, w, n_conv)

    conv_prompt = jnp.stack([g.reshape(batch, seq, d)[:, seq - nh:] for g in glu_p])
    conv_sample = jnp.stack([
        jnp.concatenate([state_conv[l], g.reshape(nseq, tq, d)], axis=1)[:, tq:] for l, g in enumerate(glu_s)])
    shape_p = (batch, seq, heads, HEAD_DIM)
    shape_s = (nseq, tq, heads, HEAD_DIM)
    return (y_p, y_s, conv_prompt, conv_sample, k_p.reshape(shape_p), v_p.reshape(shape_p),
            k_s.reshape(shape_s), v_s.reshape(shape_s))
```

```python
import functools

import jax
import jax.numpy as jnp
from jax import lax
from jax.experimental import pallas as pl
from jax.experimental.pallas import tpu as pltpu

F32 = jnp.float32
BF16 = jnp.bfloat16

EPS = 1e-6
HEAD_DIM = 64
CONV_WIDTH = 31
PAGE_SIZE = 128
PEER_HEADS = 8
PEER_TOPK = 16
PEER_NKEYS = 128
LANES = 128
BF16_ROWS = 16
MXU_DEPTH = 256
PEER_EXPERT_BLOCK = 1024
HALO = 32
VMEM_LIMIT = 56 * 1024 * 1024
NEG_INF = float("-inf")
LOG2E = 1.4426950408889634


def _params(sem, vmem=VMEM_LIMIT, flags=None):
    return pltpu.CompilerParams(dimension_semantics=sem, vmem_limit_bytes=vmem, flags=flags)


def _dot(a, b):
    return jnp.dot(a, b, preferred_element_type=F32)


def _dot_nt(a, b):
    return lax.dot_general(a, b, (((1,), (1,)), ((), ())), preferred_element_type=F32)


def _rms(x, g):
    return x * lax.rsqrt(jnp.mean(x * x, axis=-1, keepdims=True) + EPS) * g


def _softplus2(z2):
    return jnp.maximum(z2, 0.0) + jnp.log2(1.0 + jnp.exp2(-jnp.abs(z2)))


def _gelu_tanh(x):
    k = -2.0 * (2.0 / jnp.pi) ** 0.5 * LOG2E
    return x / (1.0 + jnp.exp2(x * (x * x * (k * 0.044715) + k)))


def _split_bf16(x):
    hi = x.astype(BF16)
    lo = (x - hi.astype(F32)).astype(BF16)
    return hi, lo


def _ada_kernel(c_ref, w_ref, b_ref, o_ref):
    c = c_ref[...]
    s = (c * jax.nn.sigmoid(c)).astype(BF16)
    o_ref[0] = _dot(s, w_ref[0].astype(BF16)) + b_ref[0]


def _ada(c_all, w_ada, b_ada):
    depth, d, n6 = w_ada.shape
    rows = c_all.shape[0]
    tn = 1024
    return pl.pallas_call(
        _ada_kernel,
        out_shape=jax.ShapeDtypeStruct((depth, rows, n6), F32),
        grid=(depth, n6 // tn),
        in_specs=[
            pl.BlockSpec((rows, d), lambda l, j: (0, 0)),
            pl.BlockSpec((1, d, tn), lambda l, j: (l, 0, j)),
            pl.BlockSpec((1, 1, tn), lambda l, j: (l, 0, j)),
        ],
        out_specs=pl.BlockSpec((1, rows, tn), lambda l, j: (l, 0, j)),
        compiler_params=_params(("parallel", "parallel")),
        name="ada",
    )(c_all, w_ada, b_ada.reshape(depth, 1, n6))


class _Group:
    def __init__(self, mod, per_row, rows_per_batch):
        self.mod = mod
        self.per_row = per_row
        self.rows_per_batch = rows_per_batch

    def spec(self, k, tm, d):
        if self.per_row:
            return pl.BlockSpec((1, tm, d), lambda i, *_: (k, i, 0))
        tiles = self.rows_per_batch // tm
        return pl.BlockSpec((1, 1, d), lambda i, *_: ((i // tiles) * 6 + k, 0, 0))


def _nml_kernel(*refs, has_mod, has_bias, epi, scale):
    it = iter(refs)
    x_ref, g_ref = next(it), next(it)
    sc_ref = sh_ref = b_ref = None
    if has_mod:
        sc_ref, sh_ref = next(it), next(it)
    w_ref = next(it)
    if has_bias:
        b_ref = next(it)
    outs = list(it)
    y = _rms(x_ref[...], g_ref[...])
    if has_mod:
        y = y * (1.0 + sc_ref[0]) + sh_ref[0]
    a = _dot(y.astype(BF16), w_ref[...])
    if has_bias:
        a = a + b_ref[...]
    d = x_ref.shape[1]
    if epi == "glu":
        outs[0][...] = a[:, :d] * jax.nn.sigmoid(a[:, d:])
    elif epi == "scale":
        outs[0][...] = a * scale
    else:
        k, v = a[:, :d], a[:, d:]
        outs[0][...] = k
        outs[1][...] = v
        outs[2][...] = k.astype(BF16)
        outs[3][...] = v.astype(BF16)


def _nml(x, g, w_bf16, *, group=None, mod_k=None, bias=None, epi, scale=1.0, tm=256, name):
    n, d = x.shape
    nout = w_bf16.shape[1]
    args = [x, g.reshape(1, d)]
    specs = [pl.BlockSpec((tm, d), lambda i: (i, 0)), pl.BlockSpec((1, d), lambda i: (0, 0))]
    if group is not None:
        args += [group.mod, group.mod]
        specs += [group.spec(mod_k[0], tm, d), group.spec(mod_k[1], tm, d)]
    args.append(w_bf16)
    specs.append(pl.BlockSpec((d, nout), lambda i: (0, 0)))
    if bias is not None:
        args.append(bias.reshape(1, nout))
        specs.append(pl.BlockSpec((1, nout), lambda i: (0, 0)))
    tile = pl.BlockSpec((tm, d), lambda i: (i, 0))
    if epi == "kv":
        out_shape = [jax.ShapeDtypeStruct((n, d), F32)] * 2 + [jax.ShapeDtypeStruct((n, d), BF16)] * 2
        out_specs = [tile] * 4
    else:
        out_shape = jax.ShapeDtypeStruct((n, d), F32)
        out_specs = tile
    return pl.pallas_call(
        functools.partial(_nml_kernel, has_mod=group is not None, has_bias=bias is not None,
                          epi=epi, scale=scale),
        out_shape=out_shape, grid=(n // tm,), in_specs=specs, out_specs=out_specs,
        compiler_params=_params(("parallel",)), name=name,
    )(*args)


def _gated_linear_kernel(a_ref, w_ref, x_ref, gate_ref, o_ref):
    o_ref[...] = x_ref[...] + gate_ref[0] * _dot(a_ref[...].astype(BF16), w_ref[...])


def _gated_linear(a, w_bf16, x, group, gate_k, *, tm=256, name):
    n, d = x.shape
    tile = pl.BlockSpec((tm, d), lambda i: (i, 0))
    return pl.pallas_call(
        _gated_linear_kernel,
        out_shape=jax.ShapeDtypeStruct((n, d), F32), grid=(n // tm,),
        in_specs=[tile, pl.BlockSpec((d, d), lambda i: (0, 0)), tile, group.spec(gate_k, tm, d)],
        out_specs=tile, compiler_params=_params(("parallel",)), name=name,
    )(a, w_bf16, x, group.mod)


def _conv_tail(y, lng_ref, lnb_ref, w2_ref, b2_ref, x_ref, gate_ref, o_ref):
    mu = jnp.mean(y, axis=-1, keepdims=True)
    yc = y - mu
    var = jnp.mean(yc * yc, axis=-1, keepdims=True)
    yn = yc * lax.rsqrt(var + EPS) * lng_ref[...] + lnb_ref[...]
    act = yn * jax.nn.sigmoid(yn)
    out = _dot(act.astype(BF16), w2_ref[...]) + b2_ref[...]
    o_ref[...] = x_ref[...] + gate_ref[0] * out


def _conv_prompt_kernel(glu_ref, halo_ref, wdw_ref, bdw_ref, lng_ref, lnb_ref, w2_ref, b2_ref,
                        x_ref, gate_ref, o_ref, buf_ref, y_ref):
    t = pl.program_id(1)
    tm, d = x_ref.shape
    halo = halo_ref[0]
    buf_ref[0:HALO, :] = jnp.where(t > 0, halo, jnp.zeros_like(halo))
    buf_ref[HALO:HALO + tm, :] = glu_ref[0]
    first = HALO - (CONV_WIDTH - 1)
    for c in range(d // LANES):
        cs = slice(c * LANES, (c + 1) * LANES)
        acc = jnp.broadcast_to(bdw_ref[:, cs], (tm, LANES))
        for k in range(CONV_WIDTH):
            acc = acc + wdw_ref[k:k + 1, cs] * buf_ref[first + k:first + k + tm, cs]
        y_ref[:, cs] = acc
    _conv_tail(y_ref[...], lng_ref, lnb_ref, w2_ref, b2_ref, x_ref, gate_ref, o_ref)


def _conv_prompt(glu, x, lp, group, batch, seq, *, tm=256):
    n, d = x.shape
    tiles = seq // tm
    hb = tm // HALO
    glu3 = glu.reshape(batch, seq, d)
    row = lambda b, t: (0, 0)
    return pl.pallas_call(
        _conv_prompt_kernel,
        out_shape=jax.ShapeDtypeStruct((n, d), F32), grid=(batch, tiles),
        in_specs=[
            pl.BlockSpec((1, tm, d), lambda b, t: (b, t, 0)),
            pl.BlockSpec((1, HALO, d), lambda b, t: (b, jnp.maximum(t * hb - 1, 0), 0)),
            pl.BlockSpec((HALO, d), row), pl.BlockSpec((1, d), row),
            pl.BlockSpec((1, d), row), pl.BlockSpec((1, d), row),
            pl.BlockSpec((d, d), row), pl.BlockSpec((1, d), row),
            pl.BlockSpec((tm, d), lambda b, t: (b * tiles + t, 0)),
            pl.BlockSpec((1, 1, d), lambda b, t: (b * 6 + 2, 0, 0)),
        ],
        out_specs=pl.BlockSpec((tm, d), lambda b, t: (b * tiles + t, 0)),
        scratch_shapes=[pltpu.VMEM((HALO + tm, d), F32), pltpu.VMEM((tm, d), F32)],
        compiler_params=_params(("parallel", "arbitrary")), name="conv_prompt",
    )(glu3, glu3, lp["w_dw"], lp["b_dw"], lp["ln_g"], lp["ln_b"], lp["w_pw2"], lp["b_pw2"], x, group.mod)


def _conv_sample_kernel(hist_ref, glu_ref, wdw_ref, bdw_ref, lng_ref, lnb_ref, w2_ref, b2_ref,
                        x_ref, gate_ref, o_ref, buf_ref):
    sb, tq, d = glu_ref.shape
    nh = CONV_WIDTH - 1
    buf_ref[:, 0:nh, :] = hist_ref[...]
    buf_ref[:, nh:nh + tq, :] = glu_ref[...]
    acc = jnp.broadcast_to(bdw_ref[...].reshape(1, 1, d), (sb, tq, d))
    for k in range(CONV_WIDTH):
        acc = acc + wdw_ref[k:k + 1, :].reshape(1, 1, d) * buf_ref[:, k:k + tq, :]
    _conv_tail(acc.reshape(sb * tq, d), lng_ref, lnb_ref, w2_ref, b2_ref, x_ref, gate_ref, o_ref)


def _conv_sample(glu, hist, x, lp, group, nseq, tq, *, sb=16):
    n, d = x.shape
    nh = CONV_WIDTH - 1
    row = lambda i: (0, 0)
    return pl.pallas_call(
        _conv_sample_kernel,
        out_shape=jax.ShapeDtypeStruct((n, d), F32), grid=(nseq // sb,),
        in_specs=[
            pl.BlockSpec((sb, nh, d), lambda i: (i, 0, 0)),
            pl.BlockSpec((sb, tq, d), lambda i: (i, 0, 0)),
            pl.BlockSpec((HALO, d), row), pl.BlockSpec((1, d), row),
            pl.BlockSpec((1, d), row), pl.BlockSpec((1, d), row),
            pl.BlockSpec((d, d), row), pl.BlockSpec((1, d), row),
            pl.BlockSpec((sb * tq, d), lambda i: (i, 0)),
            pl.BlockSpec((1, sb * tq, d), lambda i: (2, i, 0)),
        ],
        out_specs=pl.BlockSpec((sb * tq, d), lambda i: (i, 0)),
        scratch_shapes=[pltpu.VMEM((sb, nh + tq + 2, d), F32)],
        compiler_params=_params(("parallel",)), name="conv_sample",
    )(hist, glu.reshape(nseq, tq, d), lp["w_dw"], lp["b_dw"], lp["ln_g"], lp["ln_b"],
      lp["w_pw2"], lp["b_pw2"], x, group.mod)


def _suffix_matrix(n):
    j = lax.broadcasted_iota(jnp.int32, (n, n), 0)
    s = lax.broadcasted_iota(jnp.int32, (n, n), 1)
    return jnp.where(j > s, 1.0, 0.0).astype(BF16)


def _sb_block(z2, mask, carry, m_suffix):
    rows = z2.shape[0]
    sp = _softplus2(z2)
    if mask is not None:
        sp = jnp.where(mask, sp, 0.0)
    after = _dot(sp.astype(BF16), m_suffix)
    reps = z2.shape[1] // LANES
    carry_full = carry if reps == 1 else jnp.concatenate([carry] * reps, axis=1)
    a = jnp.exp2(z2 - sp - after - carry_full)
    if mask is not None:
        a = jnp.where(mask, a, 0.0)
    row_sum = jnp.sum(sp, axis=1, keepdims=True)
    return a, carry + jnp.broadcast_to(row_sum, carry.shape)


def _attn_prompt_kernel(bias_ref, q_ref, k_ref, v_ref, o_ref, *, kbs):
    hp = pl.program_id(1)
    qi = pl.program_id(2)
    tq = q_ref.shape[1]
    nkb = tq // kbs
    q2 = q_ref[0]
    lane = lax.broadcasted_iota(jnp.int32, (tq, LANES), 1)
    qs = jnp.concatenate([jnp.where(lane < HEAD_DIM, q2, 0.0), jnp.where(lane >= HEAD_DIM, q2, 0.0)],
                         axis=0).astype(BF16)
    row = lax.broadcasted_iota(jnp.int32, (2 * tq, kbs), 0)
    col = lax.broadcasted_iota(jnp.int32, (2 * tq, kbs), 1)
    bias = jnp.where(row < tq, bias_ref[2 * hp], bias_ref[2 * hp + 1])
    qrow = jnp.where(row < tq, row, row - tq)
    m_suffix = _suffix_matrix(kbs)

    def block(kb, mask, carry, acc):
        ks = pl.multiple_of(kb * kbs, kbs)
        z2 = _dot_nt(qs, k_ref[0, pl.ds(ks, kbs), :]) + bias
        a, carry = _sb_block(z2, mask, carry, m_suffix)
        return carry, acc + _dot(a.astype(BF16), v_ref[0, pl.ds(ks, kbs), :])

    carry = jnp.zeros((2 * tq, LANES), F32)
    acc = jnp.zeros((2 * tq, LANES), F32)
    for j in reversed(range(nkb)):
        carry, acc = block(qi * nkb + j, col + j * kbs < qrow, carry, acc)

    def body(step, c):
        kb = qi * nkb - 1 - 2 * step
        c = block(kb, None, *c)
        return block(kb - 1, None, *c)

    carry, acc = lax.fori_loop(0, qi * (nkb // 2), body, (carry, acc))
    o_ref[0] = jnp.where(lane < HEAD_DIM, acc[:tq], acc[tq:])


def _attn_prompt(q, kb, vb, bias, batch, seq, *, tq=512, kbs=256):
    n, d = q.shape
    tq = min(tq, seq)
    assert tq % (2 * kbs) == 0 and seq % tq == 0
    q3, k3, v3 = (a.reshape(batch, seq, d) for a in (q, kb, vb))
    hpairs = d // LANES
    out = pl.pallas_call(
        functools.partial(_attn_prompt_kernel, kbs=kbs),
        out_shape=jax.ShapeDtypeStruct((batch, seq, d), F32),
        grid=(batch, hpairs, seq // tq),
        in_specs=[
            pl.BlockSpec(memory_space=pltpu.SMEM),
            pl.BlockSpec((1, tq, LANES), lambda b, h, i: (b, i, h)),
            pl.BlockSpec((1, seq, LANES), lambda b, h, i: (b, 0, h)),
            pl.BlockSpec((1, seq, LANES), lambda b, h, i: (b, 0, h)),
        ],
        out_specs=pl.BlockSpec((1, tq, LANES), lambda b, h, i: (b, i, h)),
        compiler_params=_params(("parallel", "parallel", "arbitrary")), name="attn_prompt",
    )(bias, q3, k3, v3)
    return out.reshape(n, d)


def _attn_sample_kernel(pt_ref, bias_ref, m_ref, q_ref, kn_ref, vn_ref, *rest, pps):
    kp_refs, vp_refs = rest[:pps], rest[pps:2 * pps]
    o_ref, qbd_ref, knp_ref, vnp_ref, carry_ref, acc_ref = rest[2 * pps:]
    g = pl.program_id(1)
    tq, d = q_ref.shape[1], q_ref.shape[2]
    rows = (d // HEAD_DIM) * tq
    m_suffix = m_ref[0:PAGE_SIZE, 0:PAGE_SIZE]
    bias = bias_ref[...]

    @pl.when(g == 0)
    def _():
        r = lax.broadcasted_iota(jnp.int32, (rows, d), 0)
        c = lax.broadcasted_iota(jnp.int32, (rows, d), 1)
        qt = jnp.concatenate([q_ref[0]] * (rows // tq), axis=0)
        qbd_ref[...] = jnp.where(r // tq == c // HEAD_DIM, qt, 0.0).astype(BF16)
        knp_ref[...] = jnp.zeros_like(knp_ref)
        vnp_ref[...] = jnp.zeros_like(vnp_ref)
        knp_ref[0:tq, :] = kn_ref[0]
        vnp_ref[0:tq, :] = vn_ref[0]
        z = _dot_nt(qbd_ref[...], knp_ref[...].astype(BF16)) + bias
        rr = lax.broadcasted_iota(jnp.int32, (rows, PAGE_SIZE), 0)
        cc = lax.broadcasted_iota(jnp.int32, (rows, PAGE_SIZE), 1)
        mask = cc < rr % tq
        a, carry = _sb_block(z, mask, jnp.zeros((rows, LANES), F32), m_suffix)
        carry_ref[...] = carry
        acc_ref[...] = _dot(a.astype(BF16), vnp_ref[...].astype(BF16))

    kcat = jnp.concatenate([kp_refs[i][0] for i in range(pps)], axis=0)
    vcat = jnp.concatenate([vp_refs[i][0] for i in range(pps)], axis=0)
    z = _dot_nt(qbd_ref[...], kcat) + jnp.concatenate([bias] * pps, axis=1)
    a, carry = _sb_block(z, None, carry_ref[...], m_ref[...])
    carry_ref[...] = carry
    acc_ref[...] += _dot(a.astype(BF16), vcat)

    @pl.when(g == pl.num_programs(1) - 1)
    def _():
        c = lax.broadcasted_iota(jnp.int32, (tq, d), 1)
        out = jnp.zeros((tq, d), F32)
        for h in range(d // HEAD_DIM):
            out = out + jnp.where(c // HEAD_DIM == h, acc_ref[h * tq:(h + 1) * tq, :], 0.0)
        o_ref[0] = out


def _attn_sample(q, k_new, v_new, cache_k, cache_v, page_table, bias_rows, nseq, tq, *, pps=4):
    n, d = q.shape
    n_pages = page_table.shape[1]
    groups = n_pages // pps
    rows = (d // HEAD_DIM) * tq
    ck = cache_k.reshape(cache_k.shape[0], PAGE_SIZE, d).astype(BF16)
    cv = cache_v.reshape(cache_v.shape[0], PAGE_SIZE, d).astype(BF16)
    seq_spec = pl.BlockSpec((1, tq, d), lambda b, g, pt: (b, 0, 0))

    def page_spec(i):
        return pl.BlockSpec(
            (1, PAGE_SIZE, d),
            lambda b, g, pt: (pt[b * n_pages + (groups - 1 - g) * pps + i], 0, 0))

    grid_spec = pltpu.PrefetchScalarGridSpec(
        num_scalar_prefetch=1, grid=(nseq, groups),
        in_specs=[pl.BlockSpec((rows, LANES), lambda b, g, pt: (0, 0)),
                  pl.BlockSpec((pps * PAGE_SIZE, pps * PAGE_SIZE), lambda b, g, pt: (0, 0)),
                  seq_spec, seq_spec, seq_spec]
        + [page_spec(i) for i in range(pps)] * 2,
        out_specs=seq_spec,
        scratch_shapes=[pltpu.VMEM((rows, d), BF16), pltpu.VMEM((PAGE_SIZE, d), F32),
                        pltpu.VMEM((PAGE_SIZE, d), F32), pltpu.VMEM((rows, LANES), F32),
                        pltpu.VMEM((rows, d), F32)],
    )
    out = pl.pallas_call(
        functools.partial(_attn_sample_kernel, pps=pps),
        out_shape=jax.ShapeDtypeStruct((nseq, tq, d), F32), grid_spec=grid_spec,
        compiler_params=_params(("parallel", "arbitrary")), name="attn_sample",
    )(page_table.reshape(-1), bias_rows, _suffix_matrix(pps * PAGE_SIZE), q.reshape(nseq, tq, d),
      k_new.reshape(nseq, tq, d),
      v_new.reshape(nseq, tq, d), *([ck] * pps), *([cv] * pps))
    return out.reshape(n, d)


def _top16(s, want_rank):
    vals = []
    rank = jnp.full(s.shape, float(PEER_TOPK), F32)
    for r in range(PEER_TOPK):
        m = jnp.max(s, axis=0, keepdims=True)
        hit = s >= m
        if want_rank:
            rank = jnp.where(hit, float(r), rank)
        s = jnp.where(hit, NEG_INF, s)
        vals.append(m)
    return vals, rank


def _max_all(cands):
    cur = list(cands)
    while len(cur) > 1:
        cur = [jnp.maximum(cur[i], cur[i + 1]) for i in range(0, len(cur) - 1, 2)] + (
            [cur[-1]] if len(cur) % 2 else [])
    return jnp.max(cur[0], axis=0, keepdims=True)


def _select_head(s1, s2, top_ref):
    a_vals, rank1 = _top16(s1, True)
    b_vals, _ = _top16(s2, False)
    s1 = s1 - a_vals[0]
    s2 = s2 - b_vals[0]
    for r in range(PEER_TOPK):
        top_ref[0, r:r + 1, :] = a_vals[r] - a_vals[0]
        top_ref[1, r:r + 1, :] = b_vals[r] - b_vals[0]
    a_sh, b_sh = top_ref[0], top_ref[1]
    half = PEER_TOPK // 2
    sub = lax.broadcasted_iota(jnp.int32, (half, s1.shape[1]), 0)
    cand0 = [b_sh[0:half], b_sh[half:]]
    for r in range(1, half):
        cand0.append(jnp.where(sub < PEER_TOPK // (r + 1), a_sh[r:r + 1] + b_sh[0:half], NEG_INF))
    cand0.append(a_sh[half:])
    cand = list(cand0)
    for _ in range(PEER_TOPK - 1):
        m = _max_all(cand)
        cand = [jnp.where(c >= m, NEG_INF, c) for c in cand]
    tau = _max_all(cand)
    zsum = jnp.zeros_like(tau)
    for c in cand0:
        zsum = zsum + jnp.sum(jnp.where(c >= tau, jnp.exp(c), 0.0), axis=0, keepdims=True)
    m2 = jnp.where(s2 >= tau, 1.0, 0.0)
    for r in range(1, half):
        m2 = m2 + jnp.where(s2 + a_sh[r:r + 1] >= tau, 1.0, 0.0)
    deep = jnp.sum(jnp.where(a_sh[half:] >= tau, 1.0, 0.0), axis=0, keepdims=True)
    m2 = m2 + jnp.where(s2 >= 0.0, deep, 0.0)
    coef = jnp.exp(s1) / zsum
    return rank1, coef, m2.astype(BF16), jnp.exp(s2).astype(BF16)


def _peer_select_kernel(x_ref, g_ref, sc_ref, sh_ref, wq_ref, keys_ref,
                        h_ref, rank_ref, coef_ref, m2_ref, e2_ref, s_ref, top_ref):
    ts = x_ref.shape[0]
    h2 = _rms(x_ref[...], g_ref[...]) * (1.0 + sc_ref[0]) + sh_ref[0]
    h_t = h2.T.astype(BF16)
    h_ref[...] = h_t
    q_t = _dot(wq_ref[...], h_t).astype(BF16)
    for blk in range(2 * PEER_HEADS):
        s_ref[blk] = _dot(keys_ref[blk], q_t[blk * PEER_NKEYS:(blk + 1) * PEER_NKEYS, :])

    def head(h, _):
        for c in range(ts // LANES):
            cs = slice(c * LANES, (c + 1) * LANES)
            rank1, coef, m2, e2 = _select_head(s_ref[2 * h, :, cs], s_ref[2 * h + 1, :, cs], top_ref)
            rank_ref[h, :, cs] = rank1
            coef_ref[h, :, cs] = coef
            m2_ref[h, :, cs] = m2
            e2_ref[h, :, cs] = e2
        return 0

    lax.fori_loop(0, PEER_HEADS, head, 0)


def _peer_select(x, g, wq_t, keys, group, *, ts=256):
    n, d = x.shape
    nq = wq_t.shape[0]
    sel = jax.ShapeDtypeStruct((PEER_HEADS, PEER_NKEYS, n), F32)
    sel_b = jax.ShapeDtypeStruct((PEER_HEADS, PEER_NKEYS, n), BF16)
    sel_spec = pl.BlockSpec((PEER_HEADS, PEER_NKEYS, ts), lambda i: (0, 0, i))
    return pl.pallas_call(
        _peer_select_kernel,
        out_shape=[jax.ShapeDtypeStruct((d, n), BF16), sel, sel, sel_b, sel_b],
        grid=(n // ts,),
        in_specs=[
            pl.BlockSpec((ts, d), lambda i: (i, 0)), pl.BlockSpec((1, d), lambda i: (0, 0)),
            group.spec(4, ts, d), group.spec(3, ts, d),
            pl.BlockSpec((nq, d), lambda i: (0, 0)),
            pl.BlockSpec((2 * PEER_HEADS, PEER_NKEYS, PEER_NKEYS), lambda i: (0, 0, 0)),
        ],
        out_specs=[pl.BlockSpec((d, ts), lambda i: (0, i)), sel_spec, sel_spec, sel_spec, sel_spec],
        scratch_shapes=[pltpu.VMEM((2 * PEER_HEADS, PEER_NKEYS, ts), F32),
                        pltpu.VMEM((2, PEER_TOPK, LANES), F32)],
        compiler_params=_params(("parallel",)), name="peer_select",
    )(x, g.reshape(1, d), group.mod, group.mod, wq_t, keys)


def _peer_dense_kernel(h_ref, u0_ref, un_ref, vt_ref, rank_ref, coef_ref, m2_ref, e2_ref, x_ref, gate_ref,
                       o_ref, acc_ref, act_a, act_b):
    e = pl.program_id(1)
    eb = un_ref.shape[0]
    tt = h_ref.shape[1]

    @pl.when(e == 0)
    def _():
        acc_ref[...] = jnp.zeros_like(acc_ref)
        act_a[...] = _dot(u0_ref[...], h_ref[...])

    def step(act_cur, act_next):
        act_next[...] = _dot(un_ref[...], h_ref[...])
        parts = []
        for ii in range(eb // PEER_NKEYS):
            rows = slice(ii * PEER_NKEYS, (ii + 1) * PEER_NKEYS)
            w = jnp.zeros(m2_ref.shape[1:], BF16)
            for h in range(PEER_HEADS):
                r16 = jnp.broadcast_to(rank_ref[h, ii:ii + 1, :], (BF16_ROWS, tt)).astype(BF16)
                c16 = jnp.broadcast_to(coef_ref[h, ii:ii + 1, :], (BF16_ROWS, tt)).astype(BF16)
                w = w + jnp.where(r16[None] < m2_ref[h], e2_ref[h], 0) * c16[None]
            g = _gelu_tanh(act_cur[rows, :]).astype(BF16).reshape(w.shape)
            parts.append((w * g).reshape(PEER_NKEYS, tt))
        acc_ref[...] += _dot(vt_ref[0], jnp.concatenate(parts, axis=0))

    @pl.when(e % 2 == 0)
    def _():
        step(act_a, act_b)

    @pl.when(e % 2 == 1)
    def _():
        step(act_b, act_a)

    @pl.when(e == pl.num_programs(1) - 1)
    def _():
        o_ref[...] = x_ref[...] + gate_ref[0] * acc_ref[...].T


def _peer_dense(h2, u, vt, rank1, coef, m2, e2, x, group, *, tt=512):
    n, d = x.shape
    n_exp = u.shape[0]
    eb = PEER_EXPERT_BLOCK
    ib = eb // PEER_NKEYS
    last = n_exp // eb - 1
    tile = pl.BlockSpec((tt, d), lambda t, e: (t, 0))
    row_sel = pl.BlockSpec((PEER_HEADS, ib, tt), lambda t, e: (0, e, t))
    packed = (PEER_HEADS, PEER_NKEYS // BF16_ROWS, BF16_ROWS, n)
    m2, e2 = m2.reshape(packed), e2.reshape(packed)
    all_sel = pl.BlockSpec(packed[:3] + (tt,), lambda t, e: (0, 0, 0, t))
    if group.per_row:
        gate_spec = pl.BlockSpec((1, tt, d), lambda t, e: (5, t, 0))
    else:
        tiles = group.rows_per_batch // tt
        gate_spec = pl.BlockSpec((1, 1, d), lambda t, e: ((t // tiles) * 6 + 5, 0, 0))
    return pl.pallas_call(
        _peer_dense_kernel,
        out_shape=jax.ShapeDtypeStruct((n, d), F32), grid=(n // tt, n_exp // eb),
        in_specs=[
            pl.BlockSpec((d, tt), lambda t, e: (0, t)),
            pl.BlockSpec((eb, d), lambda t, e: (0, 0)),
            pl.BlockSpec((eb, d), lambda t, e: (jnp.minimum(e + 1, last), 0)),
            pl.BlockSpec((1, d, eb), lambda t, e: (e, 0, 0)),
            row_sel, row_sel, all_sel, all_sel, tile, gate_spec,
        ],
        out_specs=tile,
        scratch_shapes=[pltpu.VMEM((d, tt), F32), pltpu.VMEM((eb, tt), F32), pltpu.VMEM((eb, tt), F32)],
        compiler_params=_params(("parallel", "arbitrary")), name="peer_dense",
    )(h2, u, u, vt, rank1, coef, m2, e2, x, group.mod)


def _peer(x, g, wq_t, keys, u, vt, group):
    h2, rank1, coef, m2, e2 = _peer_select(x, g, wq_t, keys, group)
    return _peer_dense(h2, u, vt, rank1, coef, m2, e2, x, group)


def _final_norm_kernel(x_ref, g_ref, o_ref):
    o_ref[...] = _rms(x_ref[...], g_ref[...])


def _final_norm(x, g, *, tm=256):
    n, d = x.shape
    tile = pl.BlockSpec((tm, d), lambda i: (i, 0))
    return pl.pallas_call(
        _final_norm_kernel, out_shape=jax.ShapeDtypeStruct((n, d), F32), grid=(n // tm,),
        in_specs=[tile, pl.BlockSpec((1, d), lambda i: (0, 0))], out_specs=tile,
        compiler_params=_params(("parallel",)), name="final_norm",
    )(x, g.reshape(1, d))


def _trunk(x3, groups, conv_hist, past, w, n_conv):
    batch, seq, d = x3.shape
    x = x3.reshape(batch * seq, d)
    glus = []
    k_new = v_new = kb = vb = None
    for l, lp in enumerate(w["layers"]):
        group = groups[l]
        if l < n_conv:
            glu = _nml(x, lp["norm_mix"], lp["w_pw1"], group=group, mod_k=(1, 0), bias=lp["b_pw1"],
                       epi="glu", name="conv_in")
            glus.append(glu)
            if conv_hist is None:
                x = _conv_prompt(glu, x, lp, group, batch, seq)
            else:
                x = _conv_sample(glu, conv_hist[l], x, lp, group, batch, seq)
        else:
            if l == n_conv:
                k_new, v_new, kb, vb = _nml(x, w["norm_kv"], w["w_kv"], epi="kv", name="kv_proj")
            q = _nml(x, lp["norm_mix"], lp["w_q"], group=group, mod_k=(1, 0), epi="scale",
                     scale=HEAD_DIM ** -0.5 * LOG2E, name="q_proj")
            if past is None:
                att = _attn_prompt(q, kb, vb, lp["b_sb"], batch, seq)
            else:
                att = _attn_sample(q, k_new, v_new, past["cache_k"], past["cache_v"], past["page_table"],
                                   lp["bias_rows"], batch, seq)
            x = _gated_linear(att, lp["w_o"], x, group, 2, name="attn_out")
        x = _peer(x, lp["norm_ffn"], lp["w_pq_t"], lp["keys"], lp["u"], lp["vt"], group)
    y = _final_norm(x, w["final_norm"])
    return y.reshape(batch, seq, d), glus, k_new, v_new


def kernel(x_prompt, x_sample, cache_k, cache_v, state_conv, page_table, c_prompt, c_sample, w_ada, b_ada, norm_mix, norm_ffn, w_pw1, b_pw1, w_dw, b_dw, ln_g, ln_b, w_pw2, b_pw2, norm_kv, w_k, w_v, w_q, b_sb, w_o, w_pq, sub_keys, expert_u, expert_v, final_norm):
    batch, seq, d = x_prompt.shape
    nseq, tq, _ = x_sample.shape
    depth = w_ada.shape[0]
    n_conv = w_pw1.shape[0]
    heads = d // HEAD_DIM
    nh = CONV_WIDTH - 1

    rows = batch + nseq
    rows_pad = -(-rows // 8) * 8
    c_all = jnp.concatenate([c_prompt, c_sample, jnp.zeros((rows_pad - rows, d), F32)], axis=0)
    mod = _ada(c_all, w_ada, b_ada)

    def groups(l):
        mp = mod[l, :batch].reshape(batch * 6, 1, d)
        ms = jnp.repeat(mod[l, batch:rows].reshape(nseq, 6, d), tq, axis=0).transpose(1, 0, 2)
        return _Group(mp, False, seq), _Group(ms, True, tq)

    layers = []
    for l in range(depth):
        lp = {
            "norm_mix": norm_mix[l], "norm_ffn": norm_ffn[l],
            "w_pq_t": w_pq[l].T.astype(BF16),
            "keys": sub_keys[l].reshape(2 * PEER_HEADS, PEER_NKEYS, -1).astype(BF16),
            "u": expert_u[l].astype(BF16),
            "vt": expert_v[l].reshape(-1, PEER_EXPERT_BLOCK, d).transpose(0, 2, 1).astype(BF16),
        }
        if l < n_conv:
            lp.update({
                "w_pw1": w_pw1[l].astype(BF16), "b_pw1": b_pw1[l],
                "w_dw": jnp.pad(w_dw[l], ((0, HALO - CONV_WIDTH), (0, 0))),
                "b_dw": b_dw[l].reshape(1, d), "ln_g": ln_g[l].reshape(1, d), "ln_b": ln_b[l].reshape(1, d),
                "w_pw2": w_pw2[l].astype(BF16), "b_pw2": b_pw2[l].reshape(1, d),
            })
        else:
            j = l - n_conv
            lp.update({
                "w_q": w_q[j].astype(BF16), "w_o": w_o[j].astype(BF16), "b_sb": b_sb[j] * LOG2E,
                "bias_rows": jnp.broadcast_to(jnp.repeat(b_sb[j] * LOG2E, tq)[:, None], (heads * tq, LANES)),
            })
        layers.append(lp)
    w = {"norm_kv": norm_kv, "w_kv": jnp.concatenate([w_k, w_v], axis=1).astype(BF16),
         "final_norm": final_norm, "layers": layers}
    per_layer = [groups(l) for l in range(depth)]

    y_p, glu_p, k_p, v_p = _trunk(x_prompt, [g[0] for g in per_layer], None, None, w, n_conv)
    past = {"cache_k": cache_k, "cache_v": cache_v, "page_table": page_table}
    y_s, glu_s, k_s, v_s = _trunk(x_sample, [g[1] for g in per_layer], state_conv, past, w, n_conv)

    conv_prompt = jnp.stack([g.reshape(batch, seq, d)[:, seq - nh:] for g in glu_p])
    conv_sample = jnp.stack([
        jnp.concatenate([state_conv[l], g.reshape(nseq, tq, d)], axis=1)[:, tq:] for l, g in enumerate(glu_s)])
    shape_p = (batch, seq, heads, HEAD_DIM)
    shape_s = (nseq, tq, heads, HEAD_DIM)
    return (y_p, y_s, conv_prompt, conv_sample, k_p.reshape(shape_p), v_p.reshape(shape_p),
            k_s.reshape(shape_s), v_s.reshape(shape_s))
```

```python
import functools

import jax
import jax.numpy as jnp
from jax import lax
from jax.experimental import pallas as pl
from jax.experimental.pallas import tpu as pltpu

F32 = jnp.float32
BF16 = jnp.bfloat16

EPS = 1e-6
HEAD_DIM = 64
CONV_WIDTH = 31
PAGE_SIZE = 128
PEER_HEADS = 8
PEER_TOPK = 16
PEER_NKEYS = 128
LANES = 128
BF16_ROWS = 16
PEER_EXPERT_BLOCK = 1024
HALO = 32
VMEM_LIMIT = 56 * 1024 * 1024
NEG_INF = float("-inf")
LOG2E = 1.4426950408889634


def _params(sem, vmem=VMEM_LIMIT, flags=None):
    return pltpu.CompilerParams(dimension_semantics=sem, vmem_limit_bytes=vmem, flags=flags)


def _dot(a, b):
    return jnp.dot(a, b, preferred_element_type=F32)


def _dot_nt(a, b):
    return lax.dot_general(a, b, (((1,), (1,)), ((), ())), preferred_element_type=F32)


def _rms(x, g):
    return x * lax.rsqrt(jnp.mean(x * x, axis=-1, keepdims=True) + EPS) * g


def _softplus2(z2):
    return jnp.maximum(z2, 0.0) + jnp.log2(1.0 + jnp.exp2(-jnp.abs(z2)))


def _gelu_tanh(x):
    k = -2.0 * (2.0 / jnp.pi) ** 0.5 * LOG2E
    return x / (1.0 + jnp.exp2(x * (x * x * (k * 0.044715) + k)))


def _ada_kernel(c_ref, w_ref, b_ref, o_ref):
    c = c_ref[...]
    s = (c * jax.nn.sigmoid(c)).astype(BF16)
    o_ref[0] = _dot(s, w_ref[0].astype(BF16)) + b_ref[0]


def _ada(c_all, w_ada, b_ada):
    depth, d, n6 = w_ada.shape
    rows = c_all.shape[0]
    tn = 1024
    return pl.pallas_call(
        _ada_kernel,
        out_shape=jax.ShapeDtypeStruct((depth, rows, n6), F32),
        grid=(depth, n6 // tn),
        in_specs=[
            pl.BlockSpec((rows, d), lambda l, j: (0, 0)),
            pl.BlockSpec((1, d, tn), lambda l, j: (l, 0, j)),
            pl.BlockSpec((1, 1, tn), lambda l, j: (l, 0, j)),
        ],
        out_specs=pl.BlockSpec((1, rows, tn), lambda l, j: (l, 0, j)),
        compiler_params=_params(("parallel", "parallel")),
        name="ada",
    )(c_all, w_ada, b_ada.reshape(depth, 1, n6))


class _Group:
    def __init__(self, mod, per_row, rows_per_batch):
        self.mod = mod
        self.per_row = per_row
        self.rows_per_batch = rows_per_batch

    def spec(self, k, tm, d):
        if self.per_row:
            return pl.BlockSpec((1, tm, d), lambda i, *_: (k, i, 0))
        tiles = self.rows_per_batch // tm
        return pl.BlockSpec((1, 1, d), lambda i, *_: ((i // tiles) * 6 + k, 0, 0))


def _nml_kernel(*refs, has_mod, has_bias, epi, scale):
    it = iter(refs)
    x_ref, g_ref = next(it), next(it)
    sc_ref = sh_ref = b_ref = None
    if has_mod:
        sc_ref, sh_ref = next(it), next(it)
    w_ref = next(it)
    if has_bias:
        b_ref = next(it)
    outs = list(it)
    y = _rms(x_ref[...], g_ref[...])
    if has_mod:
        y = y * (1.0 + sc_ref[0]) + sh_ref[0]
    a = _dot(y.astype(BF16), w_ref[...])
    if has_bias:
        a = a + b_ref[...]
    d = x_ref.shape[1]
    if epi == "glu":
        outs[0][...] = a[:, :d] * jax.nn.sigmoid(a[:, d:])
    elif epi == "scale":
        outs[0][...] = a * scale
    else:
        k, v = a[:, :d], a[:, d:]
        outs[0][...] = k
        outs[1][...] = v
        outs[2][...] = k.astype(BF16)
        outs[3][...] = v.astype(BF16)


def _nml(x, g, w_bf16, *, group=None, mod_k=None, bias=None, epi, scale=1.0, tm=512, name):
    n, d = x.shape
    nout = w_bf16.shape[1]
    args = [x, g.reshape(1, d)]
    specs = [pl.BlockSpec((tm, d), lambda i: (i, 0)), pl.BlockSpec((1, d), lambda i: (0, 0))]
    if group is not None:
        args += [group.mod, group.mod]
        specs += [group.spec(mod_k[0], tm, d), group.spec(mod_k[1], tm, d)]
    args.append(w_bf16)
    specs.append(pl.BlockSpec((d, nout), lambda i: (0, 0)))
    if bias is not None:
        args.append(bias.reshape(1, nout))
        specs.append(pl.BlockSpec((1, nout), lambda i: (0, 0)))
    tile = pl.BlockSpec((tm, d), lambda i: (i, 0))
    if epi == "kv":
        out_shape = [jax.ShapeDtypeStruct((n, d), F32)] * 2 + [jax.ShapeDtypeStruct((n, d), BF16)] * 2
        out_specs = [tile] * 4
    else:
        out_shape = jax.ShapeDtypeStruct((n, d), F32)
        out_specs = tile
    return pl.pallas_call(
        functools.partial(_nml_kernel, has_mod=group is not None, has_bias=bias is not None,
                          epi=epi, scale=scale),
        out_shape=out_shape, grid=(n // tm,), in_specs=specs, out_specs=out_specs,
        compiler_params=_params(("parallel",)), name=name,
    )(*args)


def _gated_linear_kernel(a_ref, w_ref, x_ref, gate_ref, o_ref):
    o_ref[...] = x_ref[...] + gate_ref[0] * _dot(a_ref[...].astype(BF16), w_ref[...])


def _gated_linear(a, w_bf16, x, group, gate_k, *, tm=512, name):
    n, d = x.shape
    tile = pl.BlockSpec((tm, d), lambda i: (i, 0))
    return pl.pallas_call(
        _gated_linear_kernel,
        out_shape=jax.ShapeDtypeStruct((n, d), F32), grid=(n // tm,),
        in_specs=[tile, pl.BlockSpec((d, d), lambda i: (0, 0)), tile, group.spec(gate_k, tm, d)],
        out_specs=tile, compiler_params=_params(("parallel",)), name=name,
    )(a, w_bf16, x, group.mod)


def _conv_tail(y, lng_ref, lnb_ref, w2_ref, b2_ref, x_ref, gate_ref, o_ref):
    mu = jnp.mean(y, axis=-1, keepdims=True)
    yc = y - mu
    var = jnp.mean(yc * yc, axis=-1, keepdims=True)
    yn = yc * lax.rsqrt(var + EPS) * lng_ref[...] + lnb_ref[...]
    act = yn * jax.nn.sigmoid(yn)
    out = _dot(act.astype(BF16), w2_ref[...]) + b2_ref[...]
    o_ref[...] = x_ref[...] + gate_ref[0] * out


def _conv_prompt_kernel(glu_ref, halo_ref, wdw_ref, bdw_ref, lng_ref, lnb_ref, w2_ref, b2_ref,
                        x_ref, gate_ref, o_ref, buf_ref, y_ref):
    t = pl.program_id(1)
    tm, d = x_ref.shape
    halo = halo_ref[0]
    buf_ref[0:HALO, :] = jnp.where(t > 0, halo, jnp.zeros_like(halo))
    buf_ref[HALO:HALO + tm, :] = glu_ref[0]
    first = HALO - (CONV_WIDTH - 1)
    for c in range(d // LANES):
        cs = slice(c * LANES, (c + 1) * LANES)
        acc = jnp.broadcast_to(bdw_ref[:, cs], (tm, LANES))
        for k in range(CONV_WIDTH):
            acc = acc + wdw_ref[k:k + 1, cs] * buf_ref[first + k:first + k + tm, cs]
        y_ref[:, cs] = acc
    _conv_tail(y_ref[...], lng_ref, lnb_ref, w2_ref, b2_ref, x_ref, gate_ref, o_ref)


def _conv_prompt(glu, x, lp, group, batch, seq, *, tm=256):
    n, d = x.shape
    tiles = seq // tm
    hb = tm // HALO
    glu3 = glu.reshape(batch, seq, d)
    row = lambda b, t: (0, 0)
    return pl.pallas_call(
        _conv_prompt_kernel,
        out_shape=jax.ShapeDtypeStruct((n, d), F32), grid=(batch, tiles),
        in_specs=[
            pl.BlockSpec((1, tm, d), lambda b, t: (b, t, 0)),
            pl.BlockSpec((1, HALO, d), lambda b, t: (b, jnp.maximum(t * hb - 1, 0), 0)),
            pl.BlockSpec((HALO, d), row), pl.BlockSpec((1, d), row),
            pl.BlockSpec((1, d), row), pl.BlockSpec((1, d), row),
            pl.BlockSpec((d, d), row), pl.BlockSpec((1, d), row),
            pl.BlockSpec((tm, d), lambda b, t: (b * tiles + t, 0)),
            pl.BlockSpec((1, 1, d), lambda b, t: (b * 6 + 2, 0, 0)),
        ],
        out_specs=pl.BlockSpec((tm, d), lambda b, t: (b * tiles + t, 0)),
        scratch_shapes=[pltpu.VMEM((HALO + tm, d), F32), pltpu.VMEM((tm, d), F32)],
        compiler_params=_params(("parallel", "arbitrary")), name="conv_prompt",
    )(glu3, glu3, lp["w_dw"], lp["b_dw"], lp["ln_g"], lp["ln_b"], lp["w_pw2"], lp["b_pw2"], x, group.mod)


def _conv_sample_kernel(hist_ref, glu_ref, wdw_ref, bdw_ref, lng_ref, lnb_ref, w2_ref, b2_ref,
                        x_ref, gate_ref, o_ref, buf_ref):
    sb, tq, d = glu_ref.shape
    nh = CONV_WIDTH - 1
    buf_ref[:, 0:nh, :] = hist_ref[...]
    buf_ref[:, nh:nh + tq, :] = glu_ref[...]
    acc = jnp.broadcast_to(bdw_ref[...].reshape(1, 1, d), (sb, tq, d))
    for k in range(CONV_WIDTH):
        acc = acc + wdw_ref[k:k + 1, :].reshape(1, 1, d) * buf_ref[:, k:k + tq, :]
    _conv_tail(acc.reshape(sb * tq, d), lng_ref, lnb_ref, w2_ref, b2_ref, x_ref, gate_ref, o_ref)


def _conv_sample(glu, hist, x, lp, group, nseq, tq, *, sb=16):
    n, d = x.shape
    nh = CONV_WIDTH - 1
    row = lambda i: (0, 0)
    return pl.pallas_call(
        _conv_sample_kernel,
        out_shape=jax.ShapeDtypeStruct((n, d), F32), grid=(nseq // sb,),
        in_specs=[
            pl.BlockSpec((sb, nh, d), lambda i: (i, 0, 0)),
            pl.BlockSpec((sb, tq, d), lambda i: (i, 0, 0)),
            pl.BlockSpec((HALO, d), row), pl.BlockSpec((1, d), row),
            pl.BlockSpec((1, d), row), pl.BlockSpec((1, d), row),
            pl.BlockSpec((d, d), row), pl.BlockSpec((1, d), row),
            pl.BlockSpec((sb * tq, d), lambda i: (i, 0)),
            pl.BlockSpec((1, sb * tq, d), lambda i: (2, i, 0)),
        ],
        out_specs=pl.BlockSpec((sb * tq, d), lambda i: (i, 0)),
        scratch_shapes=[pltpu.VMEM((sb, nh + tq + 2, d), F32)],
        compiler_params=_params(("parallel",)), name="conv_sample",
    )(hist, glu.reshape(nseq, tq, d), lp["w_dw"], lp["b_dw"], lp["ln_g"], lp["ln_b"],
      lp["w_pw2"], lp["b_pw2"], x, group.mod)


def _suffix_matrix(n):
    j = lax.broadcasted_iota(jnp.int32, (n, n), 0)
    s = lax.broadcasted_iota(jnp.int32, (n, n), 1)
    return jnp.where(j > s, 1.0, 0.0).astype(BF16)


def _sb_block(z2, mask, carry, m_suffix):
    sp = _softplus2(z2)
    if mask is not None:
        sp = jnp.where(mask, sp, 0.0)
    after = _dot(sp.astype(BF16), m_suffix)
    reps = z2.shape[1] // LANES
    carry_full = carry if reps == 1 else jnp.concatenate([carry] * reps, axis=1)
    a = jnp.exp2(z2 - sp - after - carry_full)
    if mask is not None:
        a = jnp.where(mask, a, 0.0)
    row_sum = jnp.sum(sp, axis=1, keepdims=True)
    return a, carry + jnp.broadcast_to(row_sum, carry.shape)


def _attn_prompt_kernel(bias_ref, q_ref, k_ref, v_ref, o_ref, *, kbs):
    hp = pl.program_id(1)
    qi = pl.program_id(2)
    tq = q_ref.shape[1]
    nkb = tq // kbs
    q2 = q_ref[0]
    lane = lax.broadcasted_iota(jnp.int32, (tq, LANES), 1)
    qs = jnp.concatenate([jnp.where(lane < HEAD_DIM, q2, 0.0), jnp.where(lane >= HEAD_DIM, q2, 0.0)],
                         axis=0).astype(BF16)
    row = lax.broadcasted_iota(jnp.int32, (2 * tq, kbs), 0)
    col = lax.broadcasted_iota(jnp.int32, (2 * tq, kbs), 1)
    bias = jnp.where(row < tq, bias_ref[2 * hp], bias_ref[2 * hp + 1])
    qrow = jnp.where(row < tq, row, row - tq)
    m_suffix = _suffix_matrix(kbs)

    def block(kb, mask, carry, acc):
        ks = pl.multiple_of(kb * kbs, kbs)
        z2 = _dot_nt(qs, k_ref[0, pl.ds(ks, kbs), :]) + bias
        a, carry = _sb_block(z2, mask, carry, m_suffix)
        return carry, acc + _dot(a.astype(BF16), v_ref[0, pl.ds(ks, kbs), :])

    carry = jnp.zeros((2 * tq, LANES), F32)
    acc = jnp.zeros((2 * tq, LANES), F32)
    for j in reversed(range(nkb)):
        carry, acc = block(qi * nkb + j, col + j * kbs < qrow, carry, acc)

    def body(step, c):
        kb = qi * nkb - 1 - 2 * step
        c = block(kb, None, *c)
        return block(kb - 1, None, *c)

    carry, acc = lax.fori_loop(0, qi * (nkb // 2), body, (carry, acc))
    o_ref[0] = jnp.where(lane < HEAD_DIM, acc[:tq], acc[tq:])


def _attn_prompt(q, kb, vb, bias, batch, seq, *, tq=512, kbs=256):
    n, d = q.shape
    tq = min(tq, seq)
    assert tq % (2 * kbs) == 0 and seq % tq == 0
    q3, k3, v3 = (a.reshape(batch, seq, d) for a in (q, kb, vb))
    hpairs = d // LANES
    out = pl.pallas_call(
        functools.partial(_attn_prompt_kernel, kbs=kbs),
        out_shape=jax.ShapeDtypeStruct((batch, seq, d), F32),
        grid=(batch, hpairs, seq // tq),
        in_specs=[
            pl.BlockSpec(memory_space=pltpu.SMEM),
            pl.BlockSpec((1, tq, LANES), lambda b, h, i: (b, i, h)),
            pl.BlockSpec((1, seq, LANES), lambda b, h, i: (b, 0, h)),
            pl.BlockSpec((1, seq, LANES), lambda b, h, i: (b, 0, h)),
        ],
        out_specs=pl.BlockSpec((1, tq, LANES), lambda b, h, i: (b, i, h)),
        compiler_params=_params(("parallel", "parallel", "arbitrary")), name="attn_prompt",
    )(bias, q3, k3, v3)
    return out.reshape(n, d)


def _attn_sample_kernel(pt_ref, bias_ref, m_ref, q_ref, kn_ref, vn_ref, *rest, pps):
    kp_refs, vp_refs = rest[:pps], rest[pps:2 * pps]
    o_ref, qbd_ref, knp_ref, vnp_ref, carry_ref, acc_ref = rest[2 * pps:]
    g = pl.program_id(1)
    tq, d = q_ref.shape[1], q_ref.shape[2]
    rows = (d // HEAD_DIM) * tq
    m_suffix = m_ref[0:PAGE_SIZE, 0:PAGE_SIZE]
    bias = bias_ref[...]

    @pl.when(g == 0)
    def _():
        r = lax.broadcasted_iota(jnp.int32, (rows, d), 0)
        c = lax.broadcasted_iota(jnp.int32, (rows, d), 1)
        qt = jnp.concatenate([q_ref[0]] * (rows // tq), axis=0)
        qbd_ref[...] = jnp.where(r // tq == c // HEAD_DIM, qt, 0.0).astype(BF16)
        knp_ref[...] = jnp.zeros_like(knp_ref)
        vnp_ref[...] = jnp.zeros_like(vnp_ref)
        knp_ref[0:tq, :] = kn_ref[0]
        vnp_ref[0:tq, :] = vn_ref[0]
        z = _dot_nt(qbd_ref[...], knp_ref[...].astype(BF16)) + bias
        rr = lax.broadcasted_iota(jnp.int32, (rows, PAGE_SIZE), 0)
        cc = lax.broadcasted_iota(jnp.int32, (rows, PAGE_SIZE), 1)
        mask = cc < rr % tq
        a, carry = _sb_block(z, mask, jnp.zeros((rows, LANES), F32), m_suffix)
        carry_ref[...] = carry
        acc_ref[...] = _dot(a.astype(BF16), vnp_ref[...].astype(BF16))

    kcat = jnp.concatenate([kp_refs[i][0].astype(BF16) for i in range(pps)], axis=0)
    vcat = jnp.concatenate([vp_refs[i][0].astype(BF16) for i in range(pps)], axis=0)
    z = _dot_nt(qbd_ref[...], kcat) + jnp.concatenate([bias] * pps, axis=1)
    a, carry = _sb_block(z, None, carry_ref[...], m_ref[...])
    carry_ref[...] = carry
    acc_ref[...] += _dot(a.astype(BF16), vcat)

    @pl.when(g == pl.num_programs(1) - 1)
    def _():
        c = lax.broadcasted_iota(jnp.int32, (tq, d), 1)
        out = jnp.zeros((tq, d), F32)
        for h in range(d // HEAD_DIM):
            out = out + jnp.where(c // HEAD_DIM == h, acc_ref[h * tq:(h + 1) * tq, :], 0.0)
        o_ref[0] = out


def _attn_sample(q, k_new, v_new, cache_k, cache_v, page_table, bias_rows, nseq, tq, *, pps=4):
    n, d = q.shape
    n_pages = page_table.shape[1]
    groups = n_pages // pps
    rows = (d // HEAD_DIM) * tq
    ck = cache_k.reshape(cache_k.shape[0], PAGE_SIZE, d)
    cv = cache_v.reshape(cache_v.shape[0], PAGE_SIZE, d)
    seq_spec = pl.BlockSpec((1, tq, d), lambda b, g, pt: (b, 0, 0))

    def page_spec(i):
        return pl.BlockSpec(
            (1, PAGE_SIZE, d),
            lambda b, g, pt: (pt[b * n_pages + (groups - 1 - g) * pps + i], 0, 0))

    grid_spec = pltpu.PrefetchScalarGridSpec(
        num_scalar_prefetch=1, grid=(nseq, groups),
        in_specs=[pl.BlockSpec((rows, LANES), lambda b, g, pt: (0, 0)),
                  pl.BlockSpec((pps * PAGE_SIZE, pps * PAGE_SIZE), lambda b, g, pt: (0, 0)),
                  seq_spec, seq_spec, seq_spec]
        + [page_spec(i) for i in range(pps)] * 2,
        out_specs=seq_spec,
        scratch_shapes=[pltpu.VMEM((rows, d), BF16), pltpu.VMEM((PAGE_SIZE, d), F32),
                        pltpu.VMEM((PAGE_SIZE, d), F32), pltpu.VMEM((rows, LANES), F32),
                        pltpu.VMEM((rows, d), F32)],
    )
    out = pl.pallas_call(
        functools.partial(_attn_sample_kernel, pps=pps),
        out_shape=jax.ShapeDtypeStruct((nseq, tq, d), F32), grid_spec=grid_spec,
        compiler_params=_params(("parallel", "arbitrary")), name="attn_sample",
    )(page_table.reshape(-1), bias_rows, _suffix_matrix(pps * PAGE_SIZE), q.reshape(nseq, tq, d),
      k_new.reshape(nseq, tq, d),
      v_new.reshape(nseq, tq, d), *([ck] * pps), *([cv] * pps))
    return out.reshape(n, d)


def _top16(s, want_rank):
    vals = []
    rank = jnp.full(s.shape, float(PEER_TOPK), F32)
    for r in range(PEER_TOPK):
        m = jnp.max(s, axis=0, keepdims=True)
        hit = s >= m
        if want_rank:
            rank = jnp.where(hit, float(r), rank)
        s = jnp.where(hit, NEG_INF, s)
        vals.append(m)
    return vals, rank


def _max_all(cands):
    cur = list(cands)
    while len(cur) > 1:
        cur = [jnp.maximum(cur[i], cur[i + 1]) for i in range(0, len(cur) - 1, 2)] + (
            [cur[-1]] if len(cur) % 2 else [])
    return jnp.max(cur[0], axis=0, keepdims=True)


def _select_head(s1, s2, top_ref):
    a_vals, rank1 = _top16(s1, True)
    b_vals, _ = _top16(s2, False)
    s1 = s1 - a_vals[0]
    s2 = s2 - b_vals[0]
    for r in range(PEER_TOPK):
        top_ref[0, r:r + 1, :] = a_vals[r] - a_vals[0]
        top_ref[1, r:r + 1, :] = b_vals[r] - b_vals[0]
    a_sh, b_sh = top_ref[0], top_ref[1]
    half = PEER_TOPK // 2
    sub = lax.broadcasted_iota(jnp.int32, (half, s1.shape[1]), 0)
    cand0 = [b_sh[0:half], b_sh[half:]]
    for r in range(1, half):
        cand0.append(jnp.where(sub < PEER_TOPK // (r + 1), a_sh[r:r + 1] + b_sh[0:half], NEG_INF))
    cand0.append(a_sh[half:])
    cand = list(cand0)
    for _ in range(PEER_TOPK - 1):
        m = _max_all(cand)
        cand = [jnp.where(c >= m, NEG_INF, c) for c in cand]
    tau = _max_all(cand)
    zsum = jnp.zeros_like(tau)
    for c in cand0:
        zsum = zsum + jnp.sum(jnp.where(c >= tau, jnp.exp(c), 0.0), axis=0, keepdims=True)
    m2 = jnp.where(s2 >= tau, 1.0, 0.0)
    for r in range(1, half):
        m2 = m2 + jnp.where(s2 + a_sh[r:r + 1] >= tau, 1.0, 0.0)
    deep = jnp.sum(jnp.where(a_sh[half:] >= tau, 1.0, 0.0), axis=0, keepdims=True)
    m2 = m2 + jnp.where(s2 >= 0.0, deep, 0.0)
    coef = jnp.exp(s1) / zsum
    return rank1, coef, m2.astype(BF16), jnp.exp(s2).astype(BF16)


def _peer_select_kernel(x_ref, g_ref, sc_ref, sh_ref, wq_ref, keys_ref,
                        h_ref, rank_ref, coef_ref, m2_ref, e2_ref, s_ref, top_ref):
    ts = x_ref.shape[0]
    h2 = _rms(x_ref[...], g_ref[...]) * (1.0 + sc_ref[0]) + sh_ref[0]
    h_t = h2.T.astype(BF16)
    h_ref[...] = h_t
    q_t = _dot(wq_ref[...], h_t).astype(BF16)
    for blk in range(2 * PEER_HEADS):
        s_ref[blk] = _dot(keys_ref[blk], q_t[blk * PEER_NKEYS:(blk + 1) * PEER_NKEYS, :])

    def head(h, _):
        for c in range(ts // LANES):
            cs = slice(c * LANES, (c + 1) * LANES)
            rank1, coef, m2, e2 = _select_head(s_ref[2 * h, :, cs], s_ref[2 * h + 1, :, cs], top_ref)
            rank_ref[h, :, cs] = rank1
            coef_ref[h, :, cs] = coef
            m2_ref[h, :, cs] = m2
            e2_ref[h, :, cs] = e2
        return 0

    lax.fori_loop(0, PEER_HEADS, head, 0)


def _peer_select(x, g, wq_t, keys, group, *, ts=256):
    n, d = x.shape
    nq = wq_t.shape[0]
    sel = jax.ShapeDtypeStruct((PEER_HEADS, PEER_NKEYS, n), F32)
    sel_b = jax.ShapeDtypeStruct((PEER_HEADS, PEER_NKEYS, n), BF16)
    sel_spec = pl.BlockSpec((PEER_HEADS, PEER_NKEYS, ts), lambda i: (0, 0, i))
    return pl.pallas_call(
        _peer_select_kernel,
        out_shape=[jax.ShapeDtypeStruct((d, n), BF16), sel, sel, sel_b, sel_b],
        grid=(n // ts,),
        in_specs=[
            pl.BlockSpec((ts, d), lambda i: (i, 0)), pl.BlockSpec((1, d), lambda i: (0, 0)),
            group.spec(4, ts, d), group.spec(3, ts, d),
            pl.BlockSpec((nq, d), lambda i: (0, 0)),
            pl.BlockSpec((2 * PEER_HEADS, PEER_NKEYS, PEER_NKEYS), lambda i: (0, 0, 0)),
        ],
        out_specs=[pl.BlockSpec((d, ts), lambda i: (0, i)), sel_spec, sel_spec, sel_spec, sel_spec],
        scratch_shapes=[pltpu.VMEM((2 * PEER_HEADS, PEER_NKEYS, ts), F32),
                        pltpu.VMEM((2, PEER_TOPK, LANES), F32)],
        compiler_params=_params(("parallel",)), name="peer_select",
    )(x, g.reshape(1, d), group.mod, group.mod, wq_t, keys)


def _peer_dense_kernel(h_ref, u0_ref, un_ref, vt_ref, rank_ref, coef_ref, m2_ref, e2_ref, x_ref, gate_ref,
                       o_ref, acc_ref, act_a, act_b):
    e = pl.program_id(1)
    eb = un_ref.shape[0]
    tt = h_ref.shape[1]

    @pl.when(e == 0)
    def _():
        acc_ref[...] = jnp.zeros_like(acc_ref)
        act_a[...] = _dot(u0_ref[...], h_ref[...])

    def step(act_cur, act_next):
        act_next[...] = _dot(un_ref[...], h_ref[...])
        parts = []
        for ii in range(eb // PEER_NKEYS):
            rows = slice(ii * PEER_NKEYS, (ii + 1) * PEER_NKEYS)
            w = jnp.zeros(m2_ref.shape[1:], BF16)
            for h in range(PEER_HEADS):
                r16 = jnp.broadcast_to(rank_ref[h, ii:ii + 1, :], (BF16_ROWS, tt)).astype(BF16)
                c16 = jnp.broadcast_to(coef_ref[h, ii:ii + 1, :], (BF16_ROWS, tt)).astype(BF16)
                w = w + jnp.where(r16[None] < m2_ref[h], e2_ref[h], 0) * c16[None]
            g = _gelu_tanh(act_cur[rows, :]).astype(BF16).reshape(w.shape)
            parts.append((w * g).reshape(PEER_NKEYS, tt))
        acc_ref[...] += _dot(vt_ref[0], jnp.concatenate(parts, axis=0))

    @pl.when(e % 2 == 0)
    def _():
        step(act_a, act_b)

    @pl.when(e % 2 == 1)
    def _():
        step(act_b, act_a)

    @pl.when(e == pl.num_programs(1) - 1)
    def _():
        o_ref[...] = x_ref[...] + gate_ref[0] * acc_ref[...].T


def _peer_dense(h2, u, vt, rank1, coef, m2, e2, x, group, *, tt=512):
    n, d = x.shape
    n_exp = u.shape[0]
    eb = PEER_EXPERT_BLOCK
    ib = eb // PEER_NKEYS
    last = n_exp // eb - 1
    tile = pl.BlockSpec((tt, d), lambda t, e: (t, 0))
    row_sel = pl.BlockSpec((PEER_HEADS, ib, tt), lambda t, e: (0, e, t))
    packed = (PEER_HEADS, PEER_NKEYS // BF16_ROWS, BF16_ROWS, n)
    m2, e2 = m2.reshape(packed), e2.reshape(packed)
    all_sel = pl.BlockSpec(packed[:3] + (tt,), lambda t, e: (0, 0, 0, t))
    if group.per_row:
        gate_spec = pl.BlockSpec((1, tt, d), lambda t, e: (5, t, 0))
    else:
        tiles = group.rows_per_batch // tt
        gate_spec = pl.BlockSpec((1, 1, d), lambda t, e: ((t // tiles) * 6 + 5, 0, 0))
    return pl.pallas_call(
        _peer_dense_kernel,
        out_shape=jax.ShapeDtypeStruct((n, d), F32), grid=(n // tt, n_exp // eb),
        in_specs=[
            pl.BlockSpec((d, tt), lambda t, e: (0, t)),
            pl.BlockSpec((eb, d), lambda t, e: (0, 0)),
            pl.BlockSpec((eb, d), lambda t, e: (jnp.minimum(e + 1, last), 0)),
            pl.BlockSpec((1, d, eb), lambda t, e: (e, 0, 0)),
            row_sel, row_sel, all_sel, all_sel, tile, gate_spec,
        ],
        out_specs=tile,
        scratch_shapes=[pltpu.VMEM((d, tt), F32), pltpu.VMEM((eb, tt), F32), pltpu.VMEM((eb, tt), F32)],
        compiler_params=_params(("parallel", "arbitrary")), name="peer_dense",
    )(h2, u, u, vt, rank1, coef, m2, e2, x, group.mod)


def _peer(x, g, wq_t, keys, u, vt, group):
    h2, rank1, coef, m2, e2 = _peer_select(x, g, wq_t, keys, group)
    return _peer_dense(h2, u, vt, rank1, coef, m2, e2, x, group)


def _final_norm_kernel(x_ref, g_ref, o_ref):
    o_ref[...] = _rms(x_ref[...], g_ref[...])


def _final_norm(x, g, *, tm=512):
    n, d = x.shape
    tile = pl.BlockSpec((tm, d), lambda i: (i, 0))
    return pl.pallas_call(
        _final_norm_kernel, out_shape=jax.ShapeDtypeStruct((n, d), F32), grid=(n // tm,),
        in_specs=[tile, pl.BlockSpec((1, d), lambda i: (0, 0))], out_specs=tile,
        compiler_params=_params(("parallel",)), name="final_norm",
    )(x, g.reshape(1, d))


def _trunk(x3, groups, conv_hist, past, w, n_conv):
    batch, seq, d = x3.shape
    x = x3.reshape(batch * seq, d)
    glus = []
    k_new = v_new = kb = vb = None
    for l, lp in enumerate(w["layers"]):
        group = groups[l]
        if l < n_conv:
            glu = _nml(x, lp["norm_mix"], lp["w_pw1"], group=group, mod_k=(1, 0), bias=lp["b_pw1"],
                       epi="glu", name="conv_in")
            glus.append(glu)
            if conv_hist is None:
                x = _conv_prompt(glu, x, lp, group, batch, seq)
            else:
                x = _conv_sample(glu, conv_hist[l], x, lp, group, batch, seq)
        else:
            if l == n_conv:
                k_new, v_new, kb, vb = _nml(x, w["norm_kv"], w["w_kv"], epi="kv", name="kv_proj")
            q = _nml(x, lp["norm_mix"], lp["w_q"], group=group, mod_k=(1, 0), epi="scale",
                     scale=HEAD_DIM ** -0.5 * LOG2E, name="q_proj")
            if past is None:
                att = _attn_prompt(q, kb, vb, lp["b_sb"], batch, seq)
            else:
                att = _attn_sample(q, k_new, v_new, past["cache_k"], past["cache_v"], past["page_table"],
                                   lp["bias_rows"], batch, seq)
            x = _gated_linear(att, lp["w_o"], x, group, 2, name="attn_out")
        x = _peer(x, lp["norm_ffn"], lp["w_pq_t"], lp["keys"], lp["u"], lp["vt"], group)
    y = _final_norm(x, w["final_norm"])
    return y.reshape(batch, seq, d), glus, k_new, v_new


def kernel(x_prompt, x_sample, cache_k, cache_v, state_conv, page_table, c_prompt, c_sample, w_ada, b_ada, norm_mix, norm_ffn, w_pw1, b_pw1, w_dw, b_dw, ln_g, ln_b, w_pw2, b_pw2, norm_kv, w_k, w_v, w_q, b_sb, w_o, w_pq, sub_keys, expert_u, expert_v, final_norm):
    batch, seq, d = x_prompt.shape
    nseq, tq, _ = x_sample.shape
    depth = w_ada.shape[0]
    n_conv = w_pw1.shape[0]
    heads = d // HEAD_DIM
    nh = CONV_WIDTH - 1

    rows = batch + nseq
    rows_pad = -(-rows // 8) * 8
    c_all = jnp.concatenate([c_prompt, c_sample, jnp.zeros((rows_pad - rows, d), F32)], axis=0)
    mod = _ada(c_all, w_ada, b_ada)

    def groups(l):
        mp = mod[l, :batch].reshape(batch * 6, 1, d)
        ms = jnp.repeat(mod[l, batch:rows].reshape(nseq, 6, d), tq, axis=0).transpose(1, 0, 2)
        return _Group(mp, False, seq), _Group(ms, True, tq)

    layers = []
    for l in range(depth):
        lp = {
            "norm_mix": norm_mix[l], "norm_ffn": norm_ffn[l],
            "w_pq_t": w_pq[l].T.astype(BF16),
            "keys": sub_keys[l].reshape(2 * PEER_HEADS, PEER_NKEYS, -1).astype(BF16),
            "u": expert_u[l].astype(BF16),
            "vt": expert_v[l].reshape(-1, PEER_EXPERT_BLOCK, d).transpose(0, 2, 1).astype(BF16),
        }
        if l < n_conv:
            lp.update({
                "w_pw1": w_pw1[l].astype(BF16), "b_pw1": b_pw1[l],
                "w_dw": jnp.pad(w_dw[l], ((0, HALO - CONV_WIDTH), (0, 0))),
                "b_dw": b_dw[l].reshape(1, d), "ln_g": ln_g[l].reshape(1, d), "ln_b": ln_b[l].reshape(1, d),
                "w_pw2": w_pw2[l].astype(BF16), "b_pw2": b_pw2[l].reshape(1, d),
            })
        else:
            j = l - n_conv
            lp.update({
                "w_q": w_q[j].astype(BF16), "w_o": w_o[j].astype(BF16), "b_sb": b_sb[j] * LOG2E,
                "bias_rows": jnp.broadcast_to(jnp.repeat(b_sb[j] * LOG2E, tq)[:, None], (heads * tq, LANES)),
            })
        layers.append(lp)
    w = {"norm_kv": norm_kv, "w_kv": jnp.concatenate([w_k, w_v], axis=1).astype(BF16),
         "final_norm": final_norm, "layers": layers}
    per_layer = [groups(l) for l in range(depth)]

    y_p, glu_p, k_p, v_p = _trunk(x_prompt, [g[0] for g in per_layer], None, None, w, n_conv)
    past = {"cache_k": cache_k, "cache_v": cache_v, "page_table": page_table}
    y_s, glu_s, k_s, v_s = _trunk(x_sample, [g[1] for g in per_layer], state_conv, past, w, n_conv)

    conv_prompt = jnp.stack([g.reshape(batch, seq, d)[:, seq - nh:] for g in glu_p])
    conv_sample = jnp.stack([
        jnp.concatenate([state_conv[l], g.reshape(nseq, tq, d)], axis=1)[:, tq:] for l, g in enumerate(glu_s)])
    shape_p = (batch, seq, heads, HEAD_DIM)
    shape_s = (nseq, tq, heads, HEAD_DIM)
    return (y_p, y_s, conv_prompt, conv_sample, k_p.reshape(shape_p), v_p.reshape(shape_p),
            k_s.reshape(shape_s), v_s.reshape(shape_s))
```

```python
import functools

import jax
import jax.numpy as jnp
from jax import lax
from jax.experimental import pallas as pl
from jax.experimental.pallas import tpu as pltpu

F32 = jnp.float32
BF16 = jnp.bfloat16

EPS = 1e-6
HEAD_DIM = 64
CONV_WIDTH = 31
PAGE_SIZE = 128
PEER_HEADS = 8
PEER_TOPK = 16
PEER_NKEYS = 128
LANES = 128
BF16_ROWS = 16
PEER_EXPERT_BLOCK = 2048
HALO = 32
VMEM_LIMIT = 56 * 1024 * 1024
NEG_INF = float("-inf")
LOG2E = 1.4426950408889634


def _params(sem, vmem=VMEM_LIMIT, flags=None):
    return pltpu.CompilerParams(dimension_semantics=sem, vmem_limit_bytes=vmem, flags=flags)


def _dot(a, b):
    return jnp.dot(a, b, preferred_element_type=F32)


def _dot_nt(a, b):
    return lax.dot_general(a, b, (((1,), (1,)), ((), ())), preferred_element_type=F32)


def _rms(x, g):
    return x * lax.rsqrt(jnp.mean(x * x, axis=-1, keepdims=True) + EPS) * g


def _softplus2(z2):
    return jnp.maximum(z2, 0.0) + jnp.log2(1.0 + jnp.exp2(-jnp.abs(z2)))


def _gelu_tanh(x):
    k = -2.0 * (2.0 / jnp.pi) ** 0.5 * LOG2E
    return x / (1.0 + jnp.exp2(x * (x * x * (k * 0.044715) + k)))


def _ada_kernel(c_ref, w_ref, b_ref, o_ref):
    c = c_ref[...]
    s = (c * jax.nn.sigmoid(c)).astype(BF16)
    o_ref[0] = _dot(s, w_ref[0].astype(BF16)) + b_ref[0]


def _ada(c_all, w_ada, b_ada):
    depth, d, n6 = w_ada.shape
    rows = c_all.shape[0]
    tn = 1024
    return pl.pallas_call(
        _ada_kernel,
        out_shape=jax.ShapeDtypeStruct((depth, rows, n6), F32),
        grid=(depth, n6 // tn),
        in_specs=[
            pl.BlockSpec((rows, d), lambda l, j: (0, 0)),
            pl.BlockSpec((1, d, tn), lambda l, j: (l, 0, j)),
            pl.BlockSpec((1, 1, tn), lambda l, j: (l, 0, j)),
        ],
        out_specs=pl.BlockSpec((1, rows, tn), lambda l, j: (l, 0, j)),
        compiler_params=_params(("parallel", "parallel")),
        name="ada",
    )(c_all, w_ada, b_ada.reshape(depth, 1, n6))


class _Group:
    def __init__(self, mod, per_row, rows_per_batch):
        self.mod = mod
        self.per_row = per_row
        self.rows_per_batch = rows_per_batch

    def spec(self, k, tm, d):
        if self.per_row:
            return pl.BlockSpec((1, tm, d), lambda i, *_: (k, i, 0))
        tiles = self.rows_per_batch // tm
        return pl.BlockSpec((1, 1, d), lambda i, *_: ((i // tiles) * 6 + k, 0, 0))


def _nml_kernel(*refs, has_mod, has_bias, epi, scale):
    it = iter(refs)
    x_ref, g_ref = next(it), next(it)
    sc_ref = sh_ref = b_ref = None
    if has_mod:
        sc_ref, sh_ref = next(it), next(it)
    w_ref = next(it)
    if has_bias:
        b_ref = next(it)
    outs = list(it)
    y = _rms(x_ref[...], g_ref[...])
    if has_mod:
        y = y * (1.0 + sc_ref[0]) + sh_ref[0]
    a = _dot(y.astype(BF16), w_ref[...])
    if has_bias:
        a = a + b_ref[...]
    d = x_ref.shape[1]
    if epi == "glu":
        outs[0][...] = a[:, :d] * jax.nn.sigmoid(a[:, d:])
    elif epi == "scale":
        outs[0][...] = a * scale
    else:
        k, v = a[:, :d], a[:, d:]
        outs[0][...] = k
        outs[1][...] = v
        outs[2][...] = k.astype(BF16)
        outs[3][...] = v.astype(BF16)


def _nml(x, g, w_bf16, *, group=None, mod_k=None, bias=None, epi, scale=1.0, tm=512, name):
    n, d = x.shape
    nout = w_bf16.shape[1]
    args = [x, g.reshape(1, d)]
    specs = [pl.BlockSpec((tm, d), lambda i: (i, 0)), pl.BlockSpec((1, d), lambda i: (0, 0))]
    if group is not None:
        args += [group.mod, group.mod]
        specs += [group.spec(mod_k[0], tm, d), group.spec(mod_k[1], tm, d)]
    args.append(w_bf16)
    specs.append(pl.BlockSpec((d, nout), lambda i: (0, 0)))
    if bias is not None:
        args.append(bias.reshape(1, nout))
        specs.append(pl.BlockSpec((1, nout), lambda i: (0, 0)))
    tile = pl.BlockSpec((tm, d), lambda i: (i, 0))
    if epi == "kv":
        out_shape = [jax.ShapeDtypeStruct((n, d), F32)] * 2 + [jax.ShapeDtypeStruct((n, d), BF16)] * 2
        out_specs = [tile] * 4
    else:
        out_shape = jax.ShapeDtypeStruct((n, d), F32)
        out_specs = tile
    return pl.pallas_call(
        functools.partial(_nml_kernel, has_mod=group is not None, has_bias=bias is not None,
                          epi=epi, scale=scale),
        out_shape=out_shape, grid=(n // tm,), in_specs=specs, out_specs=out_specs,
        compiler_params=_params(("parallel",)), name=name,
    )(*args)


def _gated_linear_kernel(a_ref, w_ref, x_ref, gate_ref, o_ref):
    o_ref[...] = x_ref[...] + gate_ref[0] * _dot(a_ref[...].astype(BF16), w_ref[...])


def _gated_linear(a, w_bf16, x, group, gate_k, *, tm=512, name):
    n, d = x.shape
    tile = pl.BlockSpec((tm, d), lambda i: (i, 0))
    return pl.pallas_call(
        _gated_linear_kernel,
        out_shape=jax.ShapeDtypeStruct((n, d), F32), grid=(n // tm,),
        in_specs=[tile, pl.BlockSpec((d, d), lambda i: (0, 0)), tile, group.spec(gate_k, tm, d)],
        out_specs=tile, compiler_params=_params(("parallel",)), name=name,
    )(a, w_bf16, x, group.mod)


def _conv_tail(y, lng_ref, lnb_ref, w2_ref, b2_ref, x_ref, gate_ref, o_ref):
    mu = jnp.mean(y, axis=-1, keepdims=True)
    yc = y - mu
    var = jnp.mean(yc * yc, axis=-1, keepdims=True)
    yn = yc * lax.rsqrt(var + EPS) * lng_ref[...] + lnb_ref[...]
    act = yn * jax.nn.sigmoid(yn)
    out = _dot(act.astype(BF16), w2_ref[...]) + b2_ref[...]
    o_ref[...] = x_ref[...] + gate_ref[0] * out


def _conv_prompt_kernel(glu_ref, halo_ref, wdw_ref, bdw_ref, lng_ref, lnb_ref, w2_ref, b2_ref,
                        x_ref, gate_ref, o_ref, buf_ref, y_ref):
    t = pl.program_id(1)
    tm, d = x_ref.shape
    halo = halo_ref[0]
    buf_ref[0:HALO, :] = jnp.where(t > 0, halo, jnp.zeros_like(halo))
    buf_ref[HALO:HALO + tm, :] = glu_ref[0]
    first = HALO - (CONV_WIDTH - 1)
    for c in range(d // LANES):
        cs = slice(c * LANES, (c + 1) * LANES)
        acc = jnp.broadcast_to(bdw_ref[:, cs], (tm, LANES))
        for k in range(CONV_WIDTH):
            acc = acc + wdw_ref[k:k + 1, cs] * buf_ref[first + k:first + k + tm, cs]
        y_ref[:, cs] = acc
    _conv_tail(y_ref[...], lng_ref, lnb_ref, w2_ref, b2_ref, x_ref, gate_ref, o_ref)


def _conv_prompt(glu, x, lp, group, batch, seq, *, tm=256):
    n, d = x.shape
    tiles = seq // tm
    hb = tm // HALO
    glu3 = glu.reshape(batch, seq, d)
    row = lambda b, t: (0, 0)
    return pl.pallas_call(
        _conv_prompt_kernel,
        out_shape=jax.ShapeDtypeStruct((n, d), F32), grid=(batch, tiles),
        in_specs=[
            pl.BlockSpec((1, tm, d), lambda b, t: (b, t, 0)),
            pl.BlockSpec((1, HALO, d), lambda b, t: (b, jnp.maximum(t * hb - 1, 0), 0)),
            pl.BlockSpec((HALO, d), row), pl.BlockSpec((1, d), row),
            pl.BlockSpec((1, d), row), pl.BlockSpec((1, d), row),
            pl.BlockSpec((d, d), row), pl.BlockSpec((1, d), row),
            pl.BlockSpec((tm, d), lambda b, t: (b * tiles + t, 0)),
            pl.BlockSpec((1, 1, d), lambda b, t: (b * 6 + 2, 0, 0)),
        ],
        out_specs=pl.BlockSpec((tm, d), lambda b, t: (b * tiles + t, 0)),
        scratch_shapes=[pltpu.VMEM((HALO + tm, d), F32), pltpu.VMEM((tm, d), F32)],
        compiler_params=_params(("parallel", "arbitrary")), name="conv_prompt",
    )(glu3, glu3, lp["w_dw"], lp["b_dw"], lp["ln_g"], lp["ln_b"], lp["w_pw2"], lp["b_pw2"], x, group.mod)


def _conv_sample_kernel(hist_ref, glu_ref, wdw_ref, bdw_ref, lng_ref, lnb_ref, w2_ref, b2_ref,
                        x_ref, gate_ref, o_ref, buf_ref):
    sb, tq, d = glu_ref.shape
    nh = CONV_WIDTH - 1
    buf_ref[:, 0:nh, :] = hist_ref[...]
    buf_ref[:, nh:nh + tq, :] = glu_ref[...]
    acc = jnp.broadcast_to(bdw_ref[...].reshape(1, 1, d), (sb, tq, d))
    for k in range(CONV_WIDTH):
        acc = acc + wdw_ref[k:k + 1, :].reshape(1, 1, d) * buf_ref[:, k:k + tq, :]
    _conv_tail(acc.reshape(sb * tq, d), lng_ref, lnb_ref, w2_ref, b2_ref, x_ref, gate_ref, o_ref)


def _conv_sample(glu, hist, x, lp, group, nseq, tq, *, sb=16):
    n, d = x.shape
    nh = CONV_WIDTH - 1
    row = lambda i: (0, 0)
    return pl.pallas_call(
        _conv_sample_kernel,
        out_shape=jax.ShapeDtypeStruct((n, d), F32), grid=(nseq // sb,),
        in_specs=[
            pl.BlockSpec((sb, nh, d), lambda i: (i, 0, 0)),
            pl.BlockSpec((sb, tq, d), lambda i: (i, 0, 0)),
            pl.BlockSpec((HALO, d), row), pl.BlockSpec((1, d), row),
            pl.BlockSpec((1, d), row), pl.BlockSpec((1, d), row),
            pl.BlockSpec((d, d), row), pl.BlockSpec((1, d), row),
            pl.BlockSpec((sb * tq, d), lambda i: (i, 0)),
            pl.BlockSpec((1, sb * tq, d), lambda i: (2, i, 0)),
        ],
        out_specs=pl.BlockSpec((sb * tq, d), lambda i: (i, 0)),
        scratch_shapes=[pltpu.VMEM((sb, nh + tq + 2, d), F32)],
        compiler_params=_params(("parallel",)), name="conv_sample",
    )(hist, glu.reshape(nseq, tq, d), lp["w_dw"], lp["b_dw"], lp["ln_g"], lp["ln_b"],
      lp["w_pw2"], lp["b_pw2"], x, group.mod)


def _suffix_matrix(n):
    j = lax.broadcasted_iota(jnp.int32, (n, n), 0)
    s = lax.broadcasted_iota(jnp.int32, (n, n), 1)
    return jnp.where(j > s, 1.0, 0.0).astype(BF16)


def _sb_block(z2, mask, carry, m_suffix):
    sp = _softplus2(z2)
    if mask is not None:
        sp = jnp.where(mask, sp, 0.0)
    after = _dot(sp.astype(BF16), m_suffix)
    reps = z2.shape[1] // LANES
    carry_full = carry if reps == 1 else jnp.concatenate([carry] * reps, axis=1)
    a = jnp.exp2(z2 - sp - after - carry_full)
    if mask is not None:
        a = jnp.where(mask, a, 0.0)
    row_sum = jnp.sum(sp, axis=1, keepdims=True)
    return a, carry + jnp.broadcast_to(row_sum, carry.shape)


def _attn_prompt_kernel(bias_ref, q_ref, k_ref, v_ref, o_ref, *, kbs):
    hp = pl.program_id(1)
    qi = pl.program_id(2)
    tq = q_ref.shape[1]
    nkb = tq // kbs
    q2 = q_ref[0]
    lane = lax.broadcasted_iota(jnp.int32, (tq, LANES), 1)
    qs = jnp.concatenate([jnp.where(lane < HEAD_DIM, q2, 0.0), jnp.where(lane >= HEAD_DIM, q2, 0.0)],
                         axis=0).astype(BF16)
    row = lax.broadcasted_iota(jnp.int32, (2 * tq, kbs), 0)
    col = lax.broadcasted_iota(jnp.int32, (2 * tq, kbs), 1)
    bias = jnp.where(row < tq, bias_ref[2 * hp], bias_ref[2 * hp + 1])
    qrow = jnp.where(row < tq, row, row - tq)
    m_suffix = _suffix_matrix(kbs)

    def block(kb, mask, carry, acc):
        ks = pl.multiple_of(kb * kbs, kbs)
        z2 = _dot_nt(qs, k_ref[0, pl.ds(ks, kbs), :]) + bias
        a, carry = _sb_block(z2, mask, carry, m_suffix)
        return carry, acc + _dot(a.astype(BF16), v_ref[0, pl.ds(ks, kbs), :])

    carry = jnp.zeros((2 * tq, LANES), F32)
    acc = jnp.zeros((2 * tq, LANES), F32)
    for j in reversed(range(nkb)):
        carry, acc = block(qi * nkb + j, col + j * kbs < qrow, carry, acc)

    def body(step, c):
        kb = qi * nkb - 1 - 2 * step
        c = block(kb, None, *c)
        return block(kb - 1, None, *c)

    carry, acc = lax.fori_loop(0, qi * (nkb // 2), body, (carry, acc))
    o_ref[0] = jnp.where(lane < HEAD_DIM, acc[:tq], acc[tq:])


def _attn_prompt(q, kb, vb, bias, batch, seq, *, tq=512, kbs=256):
    n, d = q.shape
    tq = min(tq, seq)
    assert tq % (2 * kbs) == 0 and seq % tq == 0
    q3, k3, v3 = (a.reshape(batch, seq, d) for a in (q, kb, vb))
    hpairs = d // LANES
    out = pl.pallas_call(
        functools.partial(_attn_prompt_kernel, kbs=kbs),
        out_shape=jax.ShapeDtypeStruct((batch, seq, d), F32),
        grid=(batch, hpairs, seq // tq),
        in_specs=[
            pl.BlockSpec(memory_space=pltpu.SMEM),
            pl.BlockSpec((1, tq, LANES), lambda b, h, i: (b, i, h)),
            pl.BlockSpec((1, seq, LANES), lambda b, h, i: (b, 0, h)),
            pl.BlockSpec((1, seq, LANES), lambda b, h, i: (b, 0, h)),
        ],
        out_specs=pl.BlockSpec((1, tq, LANES), lambda b, h, i: (b, i, h)),
        compiler_params=_params(("parallel", "parallel", "arbitrary")), name="attn_prompt",
    )(bias, q3, k3, v3)
    return out.reshape(n, d)


def _attn_sample_kernel(pt_ref, bias_ref, m_ref, q_ref, kn_ref, vn_ref, *rest, pps):
    kp_refs, vp_refs = rest[:pps], rest[pps:2 * pps]
    o_ref, qbd_ref, knp_ref, vnp_ref, carry_ref, acc_ref = rest[2 * pps:]
    g = pl.program_id(1)
    tq, d = q_ref.shape[1], q_ref.shape[2]
    rows = (d // HEAD_DIM) * tq
    m_suffix = m_ref[0:PAGE_SIZE, 0:PAGE_SIZE]
    bias = bias_ref[...]

    @pl.when(g == 0)
    def _():
        r = lax.broadcasted_iota(jnp.int32, (rows, d), 0)
        c = lax.broadcasted_iota(jnp.int32, (rows, d), 1)
        qt = jnp.concatenate([q_ref[0]] * (rows // tq), axis=0)
        qbd_ref[...] = jnp.where(r // tq == c // HEAD_DIM, qt, 0.0).astype(BF16)
        knp_ref[...] = jnp.zeros_like(knp_ref)
        vnp_ref[...] = jnp.zeros_like(vnp_ref)
        knp_ref[0:tq, :] = kn_ref[0]
        vnp_ref[0:tq, :] = vn_ref[0]
        z = _dot_nt(qbd_ref[...], knp_ref[...].astype(BF16)) + bias
        rr = lax.broadcasted_iota(jnp.int32, (rows, PAGE_SIZE), 0)
        cc = lax.broadcasted_iota(jnp.int32, (rows, PAGE_SIZE), 1)
        mask = cc < rr % tq
        a, carry = _sb_block(z, mask, jnp.zeros((rows, LANES), F32), m_suffix)
        carry_ref[...] = carry
        acc_ref[...] = _dot(a.astype(BF16), vnp_ref[...].astype(BF16))

    kcat = jnp.concatenate([kp_refs[i][0].astype(BF16) for i in range(pps)], axis=0)
    vcat = jnp.concatenate([vp_refs[i][0].astype(BF16) for i in range(pps)], axis=0)
    z = _dot_nt(qbd_ref[...], kcat) + jnp.concatenate([bias] * pps, axis=1)
    a, carry = _sb_block(z, None, carry_ref[...], m_ref[...])
    carry_ref[...] = carry
    acc_ref[...] += _dot(a.astype(BF16), vcat)

    @pl.when(g == pl.num_programs(1) - 1)
    def _():
        c = lax.broadcasted_iota(jnp.int32, (tq, d), 1)
        out = jnp.zeros((tq, d), F32)
        for h in range(d // HEAD_DIM):
            out = out + jnp.where(c // HEAD_DIM == h, acc_ref[h * tq:(h + 1) * tq, :], 0.0)
        o_ref[0] = out


def _attn_sample(q, k_new, v_new, cache_k, cache_v, page_table, bias_rows, nseq, tq, *, pps=4):
    n, d = q.shape
    n_pages = page_table.shape[1]
    groups = n_pages // pps
    rows = (d // HEAD_DIM) * tq
    ck = cache_k.reshape(cache_k.shape[0], PAGE_SIZE, d)
    cv = cache_v.reshape(cache_v.shape[0], PAGE_SIZE, d)
    seq_spec = pl.BlockSpec((1, tq, d), lambda b, g, pt: (b, 0, 0))

    def page_spec(i):
        return pl.BlockSpec(
            (1, PAGE_SIZE, d),
            lambda b, g, pt: (pt[b * n_pages + (groups - 1 - g) * pps + i], 0, 0))

    grid_spec = pltpu.PrefetchScalarGridSpec(
        num_scalar_prefetch=1, grid=(nseq, groups),
        in_specs=[pl.BlockSpec((rows, LANES), lambda b, g, pt: (0, 0)),
                  pl.BlockSpec((pps * PAGE_SIZE, pps * PAGE_SIZE), lambda b, g, pt: (0, 0)),
                  seq_spec, seq_spec, seq_spec]
        + [page_spec(i) for i in range(pps)] * 2,
        out_specs=seq_spec,
        scratch_shapes=[pltpu.VMEM((rows, d), BF16), pltpu.VMEM((PAGE_SIZE, d), F32),
                        pltpu.VMEM((PAGE_SIZE, d), F32), pltpu.VMEM((rows, LANES), F32),
                        pltpu.VMEM((rows, d), F32)],
    )
    out = pl.pallas_call(
        functools.partial(_attn_sample_kernel, pps=pps),
        out_shape=jax.ShapeDtypeStruct((nseq, tq, d), F32), grid_spec=grid_spec,
        compiler_params=_params(("parallel", "arbitrary")), name="attn_sample",
    )(page_table.reshape(-1), bias_rows, _suffix_matrix(pps * PAGE_SIZE), q.reshape(nseq, tq, d),
      k_new.reshape(nseq, tq, d),
      v_new.reshape(nseq, tq, d), *([ck] * pps), *([cv] * pps))
    return out.reshape(n, d)


def _top16(s, want_rank):
    vals = []
    rank = jnp.full(s.shape, float(PEER_TOPK), F32)
    for r in range(PEER_TOPK):
        m = jnp.max(s, axis=0, keepdims=True)
        hit = s >= m
        if want_rank:
            rank = jnp.where(hit, float(r), rank)
        s = jnp.where(hit, NEG_INF, s)
        vals.append(m)
    return vals, rank


def _max_all(cands):
    cur = list(cands)
    while len(cur) > 1:
        cur = [jnp.maximum(cur[i], cur[i + 1]) for i in range(0, len(cur) - 1, 2)] + (
            [cur[-1]] if len(cur) % 2 else [])
    return jnp.max(cur[0], axis=0, keepdims=True)


def _select_head(s1, s2, top_ref):
    a_vals, rank1 = _top16(s1, True)
    b_vals, _ = _top16(s2, False)
    s1 = s1 - a_vals[0]
    s2 = s2 - b_vals[0]
    for r in range(PEER_TOPK):
        top_ref[0, r:r + 1, :] = a_vals[r] - a_vals[0]
        top_ref[1, r:r + 1, :] = b_vals[r] - b_vals[0]
    a_sh, b_sh = top_ref[0], top_ref[1]
    half = PEER_TOPK // 2
    sub = lax.broadcasted_iota(jnp.int32, (half, s1.shape[1]), 0)
    cand0 = [b_sh[0:half], b_sh[half:]]
    for r in range(1, half):
        cand0.append(jnp.where(sub < PEER_TOPK // (r + 1), a_sh[r:r + 1] + b_sh[0:half], NEG_INF))
    cand0.append(a_sh[half:])
    cand = list(cand0)
    for _ in range(PEER_TOPK - 1):
        m = _max_all(cand)
        cand = [jnp.where(c >= m, NEG_INF, c) for c in cand]
    tau = _max_all(cand)
    zsum = jnp.zeros_like(tau)
    for c in cand0:
        zsum = zsum + jnp.sum(jnp.where(c >= tau, jnp.exp(c), 0.0), axis=0, keepdims=True)
    m2 = jnp.where(s2 >= tau, 1.0, 0.0)
    for r in range(1, half):
        m2 = m2 + jnp.where(s2 + a_sh[r:r + 1] >= tau, 1.0, 0.0)
    deep = jnp.sum(jnp.where(a_sh[half:] >= tau, 1.0, 0.0), axis=0, keepdims=True)
    m2 = m2 + jnp.where(s2 >= 0.0, deep, 0.0)
    coef = jnp.exp(s1) / zsum
    return rank1, coef, m2.astype(BF16), jnp.exp(s2).astype(BF16)


def _peer_select_kernel(x_ref, g_ref, sc_ref, sh_ref, wq_ref, keys_ref,
                        h_ref, rank_ref, coef_ref, m2_ref, e2_ref, s_ref, top_ref):
    ts = x_ref.shape[0]
    h2 = _rms(x_ref[...], g_ref[...]) * (1.0 + sc_ref[0]) + sh_ref[0]
    h_t = h2.T.astype(BF16)
    h_ref[...] = h_t
    q_t = _dot(wq_ref[...], h_t).astype(BF16)
    for blk in range(2 * PEER_HEADS):
        s_ref[blk] = _dot(keys_ref[blk], q_t[blk * PEER_NKEYS:(blk + 1) * PEER_NKEYS, :])

    def head(h, _):
        for c in range(ts // LANES):
            cs = slice(c * LANES, (c + 1) * LANES)
            rank1, coef, m2, e2 = _select_head(s_ref[2 * h, :, cs], s_ref[2 * h + 1, :, cs], top_ref)
            rank_ref[h, :, cs] = rank1
            coef_ref[h, :, cs] = coef
            m2_ref[h, :, cs] = m2
            e2_ref[h, :, cs] = e2
        return 0

    lax.fori_loop(0, PEER_HEADS, head, 0)


def _peer_select(x, g, wq_t, keys, group, *, ts=256):
    n, d = x.shape
    nq = wq_t.shape[0]
    sel = jax.ShapeDtypeStruct((PEER_HEADS, PEER_NKEYS, n), F32)
    sel_b = jax.ShapeDtypeStruct((PEER_HEADS, PEER_NKEYS, n), BF16)
    sel_spec = pl.BlockSpec((PEER_HEADS, PEER_NKEYS, ts), lambda i: (0, 0, i))
    return pl.pallas_call(
        _peer_select_kernel,
        out_shape=[jax.ShapeDtypeStruct((d, n), BF16), sel, sel, sel_b, sel_b],
        grid=(n // ts,),
        in_specs=[
            pl.BlockSpec((ts, d), lambda i: (i, 0)), pl.BlockSpec((1, d), lambda i: (0, 0)),
            group.spec(4, ts, d), group.spec(3, ts, d),
            pl.BlockSpec((nq, d), lambda i: (0, 0)),
            pl.BlockSpec((2 * PEER_HEADS, PEER_NKEYS, PEER_NKEYS), lambda i: (0, 0, 0)),
        ],
        out_specs=[pl.BlockSpec((d, ts), lambda i: (0, i)), sel_spec, sel_spec, sel_spec, sel_spec],
        scratch_shapes=[pltpu.VMEM((2 * PEER_HEADS, PEER_NKEYS, ts), F32),
                        pltpu.VMEM((2, PEER_TOPK, LANES), F32)],
        compiler_params=_params(("parallel",)), name="peer_select",
    )(x, g.reshape(1, d), group.mod, group.mod, wq_t, keys)


def _peer_dense_kernel(h_ref, u0_ref, un_ref, vt_ref, rank_ref, coef_ref, m2_ref, e2_ref, x_ref, gate_ref,
                       o_ref, acc_ref, act_a, act_b):
    e = pl.program_id(1)
    eb = un_ref.shape[0]
    tt = h_ref.shape[1]

    @pl.when(e == 0)
    def _():
        acc_ref[...] = jnp.zeros_like(acc_ref)
        act_a[...] = _dot(u0_ref[...], h_ref[...])

    def step(act_cur, act_next):
        act_next[...] = _dot(un_ref[...], h_ref[...])
        parts = []
        for ii in range(eb // PEER_NKEYS):
            rows = slice(ii * PEER_NKEYS, (ii + 1) * PEER_NKEYS)
            w = jnp.zeros(m2_ref.shape[1:], BF16)
            for h in range(PEER_HEADS):
                r16 = jnp.broadcast_to(rank_ref[h, ii:ii + 1, :], (BF16_ROWS, tt)).astype(BF16)
                c16 = jnp.broadcast_to(coef_ref[h, ii:ii + 1, :], (BF16_ROWS, tt)).astype(BF16)
                w = w + jnp.where(r16[None] < m2_ref[h], e2_ref[h], 0) * c16[None]
            g = _gelu_tanh(act_cur[rows, :]).astype(BF16).reshape(w.shape)
            parts.append((w * g).reshape(PEER_NKEYS, tt))
        acc_ref[...] += _dot(vt_ref[0], jnp.concatenate(parts, axis=0))

    @pl.when(e % 2 == 0)
    def _():
        step(act_a, act_b)

    @pl.when(e % 2 == 1)
    def _():
        step(act_b, act_a)

    @pl.when(e == pl.num_programs(1) - 1)
    def _():
        o_ref[...] = x_ref[...] + gate_ref[0] * acc_ref[...].T


def _peer_dense(h2, u, vt, rank1, coef, m2, e2, x, group, *, tt=512):
    n, d = x.shape
    n_exp = u.shape[0]
    eb = PEER_EXPERT_BLOCK
    ib = eb // PEER_NKEYS
    last = n_exp // eb - 1
    tile = pl.BlockSpec((tt, d), lambda t, e: (t, 0))
    row_sel = pl.BlockSpec((PEER_HEADS, ib, tt), lambda t, e: (0, e, t))
    packed = (PEER_HEADS, PEER_NKEYS // BF16_ROWS, BF16_ROWS, n)
    m2, e2 = m2.reshape(packed), e2.reshape(packed)
    all_sel = pl.BlockSpec(packed[:3] + (tt,), lambda t, e: (0, 0, 0, t))
    if group.per_row:
        gate_spec = pl.BlockSpec((1, tt, d), lambda t, e: (5, t, 0))
    else:
        tiles = group.rows_per_batch // tt
        gate_spec = pl.BlockSpec((1, 1, d), lambda t, e: ((t // tiles) * 6 + 5, 0, 0))
    return pl.pallas_call(
        _peer_dense_kernel,
        out_shape=jax.ShapeDtypeStruct((n, d), F32), grid=(n // tt, n_exp // eb),
        in_specs=[
            pl.BlockSpec((d, tt), lambda t, e: (0, t)),
            pl.BlockSpec((eb, d), lambda t, e: (0, 0)),
            pl.BlockSpec((eb, d), lambda t, e: (jnp.minimum(e + 1, last), 0)),
            pl.BlockSpec((1, d, eb), lambda t, e: (e, 0, 0)),
            row_sel, row_sel, all_sel, all_sel, tile, gate_spec,
        ],
        out_specs=tile,
        scratch_shapes=[pltpu.VMEM((d, tt), F32), pltpu.VMEM((eb, tt), F32), pltpu.VMEM((eb, tt), F32)],
        compiler_params=_params(("parallel", "arbitrary")), name="peer_dense",
    )(h2, u, u, vt, rank1, coef, m2, e2, x, group.mod)


def _peer(x, g, wq_t, keys, u, vt, group):
    h2, rank1, coef, m2, e2 = _peer_select(x, g, wq_t, keys, group)
    return _peer_dense(h2, u, vt, rank1, coef, m2, e2, x, group)


def _final_norm_kernel(x_ref, g_ref, o_ref):
    o_ref[...] = _rms(x_ref[...], g_ref[...])


def _final_norm(x, g, *, tm=512):
    n, d = x.shape
    tile = pl.BlockSpec((tm, d), lambda i: (i, 0))
    return pl.pallas_call(
        _final_norm_kernel, out_shape=jax.ShapeDtypeStruct((n, d), F32), grid=(n // tm,),
        in_specs=[tile, pl.BlockSpec((1, d), lambda i: (0, 0))], out_specs=tile,
        compiler_params=_params(("parallel",)), name="final_norm",
    )(x, g.reshape(1, d))


def _trunk(x3, groups, conv_hist, past, w, n_conv):
    batch, seq, d = x3.shape
    x = x3.reshape(batch * seq, d)
    glus = []
    k_new = v_new = kb = vb = None
    for l, lp in enumerate(w["layers"]):
        group = groups[l]
        if l < n_conv:
            glu = _nml(x, lp["norm_mix"], lp["w_pw1"], group=group, mod_k=(1, 0), bias=lp["b_pw1"],
                       epi="glu", name="conv_in")
            glus.append(glu)
            if conv_hist is None:
                x = _conv_prompt(glu, x, lp, group, batch, seq)
            else:
                x = _conv_sample(glu, conv_hist[l], x, lp, group, batch, seq)
        else:
            if l == n_conv:
                k_new, v_new, kb, vb = _nml(x, w["norm_kv"], w["w_kv"], epi="kv", name="kv_proj")
            q = _nml(x, lp["norm_mix"], lp["w_q"], group=group, mod_k=(1, 0), epi="scale",
                     scale=HEAD_DIM ** -0.5 * LOG2E, name="q_proj")
            if past is None:
                att = _attn_prompt(q, kb, vb, lp["b_sb"], batch, seq)
            else:
                att = _attn_sample(q, k_new, v_new, past["cache_k"], past["cache_v"], past["page_table"],
                                   lp["bias_rows"], batch, seq)
            x = _gated_linear(att, lp["w_o"], x, group, 2, name="attn_out")
        x = _peer(x, lp["norm_ffn"], lp["w_pq_t"], lp["keys"], lp["u"], lp["vt"], group)
    y = _final_norm(x, w["final_norm"])
    return y.reshape(batch, seq, d), glus, k_new, v_new


def kernel(x_prompt, x_sample, cache_k, cache_v, state_conv, page_table, c_prompt, c_sample, w_ada, b_ada, norm_mix, norm_ffn, w_pw1, b_pw1, w_dw, b_dw, ln_g, ln_b, w_pw2, b_pw2, norm_kv, w_k, w_v, w_q, b_sb, w_o, w_pq, sub_keys, expert_u, expert_v, final_norm):
    batch, seq, d = x_prompt.shape
    nseq, tq, _ = x_sample.shape
    depth = w_ada.shape[0]
    n_conv = w_pw1.shape[0]
    heads = d // HEAD_DIM
    nh = CONV_WIDTH - 1

    rows = batch + nseq
    rows_pad = -(-rows // 8) * 8
    c_all = jnp.concatenate([c_prompt, c_sample, jnp.zeros((rows_pad - rows, d), F32)], axis=0)
    mod = _ada(c_all, w_ada, b_ada)

    def groups(l):
        mp = mod[l, :batch].reshape(batch * 6, 1, d)
        ms = jnp.repeat(mod[l, batch:rows].reshape(nseq, 6, d), tq, axis=0).transpose(1, 0, 2)
        return _Group(mp, False, seq), _Group(ms, True, tq)

    layers = []
    for l in range(depth):
        lp = {
            "norm_mix": norm_mix[l], "norm_ffn": norm_ffn[l],
            "w_pq_t": w_pq[l].T.astype(BF16),
            "keys": sub_keys[l].reshape(2 * PEER_HEADS, PEER_NKEYS, -1).astype(BF16),
            "u": expert_u[l].astype(BF16),
            "vt": expert_v[l].reshape(-1, PEER_EXPERT_BLOCK, d).transpose(0, 2, 1).astype(BF16),
        }
        if l < n_conv:
            lp.update({
                "w_pw1": w_pw1[l].astype(BF16), "b_pw1": b_pw1[l],
                "w_dw": jnp.pad(w_dw[l], ((0, HALO - CONV_WIDTH), (0, 0))),
                "b_dw": b_dw[l].reshape(1, d), "ln_g": ln_g[l].reshape(1, d), "ln_b": ln_b[l].reshape(1, d),
                "w_pw2": w_pw2[l].astype(BF16), "b_pw2": b_pw2[l].reshape(1, d),
            })
        else:
            j = l - n_conv
            lp.update({
                "w_q": w_q[j].astype(BF16), "w_o": w_o[j].astype(BF16), "b_sb": b_sb[j] * LOG2E,
                "bias_rows": jnp.broadcast_to(jnp.repeat(b_sb[j] * LOG2E, tq)[:, None], (heads * tq, LANES)),
            })
        layers.append(lp)
    w = {"norm_kv": norm_kv, "w_kv": jnp.concatenate([w_k, w_v], axis=1).astype(BF16),
         "final_norm": final_norm, "layers": layers}
    per_layer = [groups(l) for l in range(depth)]

    y_p, glu_p, k_p, v_p = _trunk(x_prompt, [g[0] for g in per_layer], None, None, w, n_conv)
    past = {"cache_k": cache_k, "cache_v": cache_v, "page_table": page_table}
    y_s, glu_s, k_s, v_s = _trunk(x_sample, [g[1] for g in per_layer], state_conv, past, w, n_conv)

    conv_prompt = jnp.stack([g.reshape(batch, seq, d)[:, seq - nh:] for g in glu_p])
    conv_sample = jnp.stack([
        jnp.concatenate([state_conv[l], g.reshape(nseq, tq, d)], axis=1)[:, tq:] for l, g in enumerate(glu_s)])
    shape_p = (batch, seq, heads, HEAD_DIM)
    shape_s = (nseq, tq, heads, HEAD_DIM)
    return (y_p, y_s, conv_prompt, conv_sample, k_p.reshape(shape_p), v_p.reshape(shape_p),
            k_s.reshape(shape_s), v_s.reshape(shape_s))
```
